```python
import math
import jax, jax.numpy as jnp
from jax import lax
import numpy as np

D_MODEL = 1024
BATCH = 4
SEQ = 4096
DEPTH = 2

CHUNK = 64
Q_BLOCK = 128
N_MIXERS = 2
N_DIFF_LAYERS = (DEPTH + N_MIXERS - 1) // N_MIXERS
N_MLA_LAYERS = DEPTH // N_MIXERS
DA_HEAD_DIM = 64
DA_HEADS = D_MODEL // (2 * DA_HEAD_DIM)
MLA_HEADS = 16
MLA_NOPE = 64
MLA_ROPE = 32
MLA_V = 64
MLA_Q_LORA = 512
MLA_KV_LORA = 256
ROPE_THETA = 10000.0
REL_BUCKETS = 32
REL_MAX_DIST = 128
D_FF = 4 * D_MODEL
DEEPNORM_ALPHA = (2 * DEPTH) ** 0.25
DEEPNORM_BETA = (8 * DEPTH) ** -0.25
LN_EPS = 1e-5
RMS_EPS = 1e-6

kernel_name = 'hybrid_diffattn_mla_deepnorm_adaln_encoder'

F32 = jnp.float32


def _layer_norm(x, g, b):
    xf = x.astype(F32)
    mu = jnp.mean(xf, axis=-1, keepdims=True)
    var = jnp.mean(jnp.square(xf - mu), axis=-1, keepdims=True)
    return ((xf - mu) * lax.rsqrt(var + LN_EPS) * g.astype(F32) + b.astype(F32)).astype(x.dtype)


def _rms_norm(x, w):
    xf = x.astype(F32)
    y = xf * lax.rsqrt(jnp.mean(xf * xf, axis=-1, keepdims=True) + RMS_EPS)
    return (y * w.astype(F32)).astype(x.dtype)


def _chunk_mask(q0, n_keys):
    q_chunk = (q0 + jnp.arange(Q_BLOCK)) // CHUNK
    k_chunk = jnp.arange(n_keys) // CHUNK
    return k_chunk[None, :] <= q_chunk[:, None]


def _t5_bucket(rel):
    nb = REL_BUCKETS // 2
    ret = (rel > 0).astype(jnp.int32) * nb
    n = jnp.abs(rel)
    max_exact = nb // 2
    is_small = n < max_exact
    n_f = jnp.maximum(n, 1).astype(F32)
    large = max_exact + (jnp.log(n_f / max_exact) / math.log(REL_MAX_DIST / max_exact)
                         * (nb - max_exact)).astype(jnp.int32)
    large = jnp.minimum(large, nb - 1)
    return ret + jnp.where(is_small, n, large)


def _rel_bias(q0, n_keys, rel_table):
    rel = jnp.arange(n_keys)[None, :] - (q0 + jnp.arange(Q_BLOCK))[:, None]
    bias = rel_table[_t5_bucket(rel)].astype(F32)
    return jnp.transpose(bias, (2, 0, 1))


def _rope(x, pos):
    half = x.shape[-1] // 2
    inv = ROPE_THETA ** (-jnp.arange(half, dtype=F32) / half)
    ang = pos.astype(F32)[..., None] * inv
    cos = jnp.cos(ang)[:, :, None, :]
    sin = jnp.sin(ang)[:, :, None, :]
    xf = x.astype(F32)
    x1, x2 = xf[..., :half], xf[..., half:]
    return jnp.concatenate([x1 * cos - x2 * sin, x1 * sin + x2 * cos], axis=-1).astype(x.dtype)


def _masked_softmax(s, mask):
    return jax.nn.softmax(jnp.where(mask, s, -jnp.inf), axis=-1)


def diff_attention(u, w_qkv, w_o, lam_q1, lam_k1, lam_q2, lam_k2, subln_w, rel_table, layer_idx):
    B, S, _ = u.shape
    qkv = u @ w_qkv
    q = qkv[..., :D_MODEL].reshape(B, S, DA_HEADS, 2, DA_HEAD_DIM)
    k = qkv[..., D_MODEL:2 * D_MODEL].reshape(B, S, DA_HEADS, 2, DA_HEAD_DIM)
    v = qkv[..., 2 * D_MODEL:].reshape(B, S, DA_HEADS, 2 * DA_HEAD_DIM)
    lam_init = 0.8 - 0.6 * math.exp(-0.3 * layer_idx)
    lam = (jnp.exp(jnp.sum(lam_q1.astype(F32) * lam_k1.astype(F32)))
           - jnp.exp(jnp.sum(lam_q2.astype(F32) * lam_k2.astype(F32))) + lam_init)
    scale = DA_HEAD_DIM ** -0.5
    outs = []
    for qb in range(S // Q_BLOCK):
        q0 = qb * Q_BLOCK
        n_keys = q0 + Q_BLOCK
        s = jnp.einsum('bqhcd,bkhcd->bhcqk', q[:, q0:n_keys], k[:, :n_keys]).astype(F32) * scale
        s = s + _rel_bias(q0, n_keys, rel_table)[None, :, None]
        p = _masked_softmax(s, _chunk_mask(q0, n_keys)[None, None, None])
        a = p[:, :, 0] - lam * p[:, :, 1]
        outs.append(jnp.einsum('bhqk,bkhe->bqhe', a.astype(v.dtype), v[:, :n_keys]))
    o = jnp.concatenate(outs, axis=1)
    o = _rms_norm(o, subln_w) * (1.0 - lam_init)
    return o.reshape(B, S, D_MODEL) @ w_o


def mla_attention(u, pos, w_down, q_norm_w, w_uq, kv_norm_w, w_ukv, w_o):
    B, S, _ = u.shape
    down = u @ w_down
    c_q = _rms_norm(down[..., :MLA_Q_LORA], q_norm_w)
    c_kv = _rms_norm(down[..., MLA_Q_LORA:MLA_Q_LORA + MLA_KV_LORA], kv_norm_w)
    k_rope = _rope(down[..., MLA_Q_LORA + MLA_KV_LORA:][:, :, None, :], pos)[:, :, 0]
    q = (c_q @ w_uq).reshape(B, S, MLA_HEADS, MLA_NOPE + MLA_ROPE)
    q_nope = q[..., :MLA_NOPE]
    q_rope = _rope(q[..., MLA_NOPE:], pos)
    kv = (c_kv @ w_ukv).reshape(B, S, MLA_HEADS, MLA_NOPE + MLA_V)
    k_nope, v = kv[..., :MLA_NOPE], kv[..., MLA_NOPE:]
    scale = (MLA_NOPE + MLA_ROPE) ** -0.5
    outs = []
    for qb in range(S // Q_BLOCK):
        q0 = qb * Q_BLOCK
        n_keys = q0 + Q_BLOCK
        s = (jnp.einsum('bqhd,bkhd->bhqk', q_nope[:, q0:n_keys], k_nope[:, :n_keys])
             + jnp.einsum('bqhr,bkr->bhqk', q_rope[:, q0:n_keys], k_rope[:, :n_keys])).astype(F32) * scale
        p = _masked_softmax(s, _chunk_mask(q0, n_keys)[None, None])
        outs.append(jnp.einsum('bhqk,bkhe->bqhe', p.astype(v.dtype), v[:, :n_keys]))
    o = jnp.concatenate(outs, axis=1)
    return o.reshape(B, S, MLA_HEADS * MLA_V) @ w_o


def squared_relu_mlp(u, w1, w2):
    return jnp.square(jax.nn.relu(u @ w1)) @ w2


def setup_inputs(seed: int = 0) -> dict:
    key = jax.random.key(seed)
    ks = jax.random.split(key, 28)
    D = D_MODEL
    beta = DEEPNORM_BETA

    def nrm(k, shape, std):
        return jax.random.normal(k, shape, F32) * std

    x = nrm(ks[0], (BATCH, SEQ, D), 1.0)
    c = nrm(ks[1], (BATCH, D), 1.0)
    pos_offset = (jax.random.randint(ks[2], (BATCH,), 0, 64) * CHUNK).astype(jnp.int32)
    ada_w = nrm(ks[3], (DEPTH, D, 6 * D), 0.25 * D ** -0.5)
    ada_b = nrm(ks[4], (DEPTH, 6 * D), 0.02)
    ln_g = 1.0 + nrm(ks[5], (DEPTH, 2, D), 0.02)
    ln_b = nrm(ks[6], (DEPTH, 2, D), 0.02)
    rel_table = nrm(ks[7], (REL_BUCKETS, DA_HEADS), 0.5)
    da_w_qk = nrm(ks[8], (N_DIFF_LAYERS, D, 2 * D), D ** -0.5)
    da_w_v = nrm(ks[9], (N_DIFF_LAYERS, D, D), beta * D ** -0.5)
    da_w_qkv = jnp.concatenate([da_w_qk, da_w_v], axis=-1)
    da_w_o = nrm(ks[10], (N_DIFF_LAYERS, D, D), beta * D ** -0.5)
    da_lam_q1 = nrm(ks[11], (N_DIFF_LAYERS, DA_HEAD_DIM), 0.1)
    da_lam_k1 = nrm(ks[12], (N_DIFF_LAYERS, DA_HEAD_DIM), 0.1)
    da_lam_q2 = nrm(ks[13], (N_DIFF_LAYERS, DA_HEAD_DIM), 0.1)
    da_lam_k2 = nrm(ks[14], (N_DIFF_LAYERS, DA_HEAD_DIM), 0.1)
    da_subln_w = 1.0 + nrm(ks[15], (N_DIFF_LAYERS, 2 * DA_HEAD_DIM), 0.02)
    mla_w_down = nrm(ks[16], (N_MLA_LAYERS, D, MLA_Q_LORA + MLA_KV_LORA + MLA_ROPE), D ** -0.5)
    mla_q_norm_w = 1.0 + nrm(ks[17], (N_MLA_LAYERS, MLA_Q_LORA), 0.02)
    mla_w_uq = nrm(ks[18], (N_MLA_LAYERS, MLA_Q_LORA, MLA_HEADS * (MLA_NOPE + MLA_ROPE)), MLA_Q_LORA ** -0.5)
    mla_kv_norm_w = 1.0 + nrm(ks[19], (N_MLA_LAYERS, MLA_KV_LORA), 0.02)
    w_uk = nrm(ks[20], (N_MLA_LAYERS, MLA_KV_LORA, MLA_HEADS, MLA_NOPE), MLA_KV_LORA ** -0.5)
    w_uv = nrm(ks[21], (N_MLA_LAYERS, MLA_KV_LORA, MLA_HEADS, MLA_V), beta * MLA_KV_LORA ** -0.5)
    mla_w_ukv = jnp.concatenate([w_uk, w_uv], axis=-1).reshape(
        N_MLA_LAYERS, MLA_KV_LORA, MLA_HEADS * (MLA_NOPE + MLA_V))
    mla_w_o = nrm(ks[22], (N_MLA_LAYERS, MLA_HEADS * MLA_V, D), beta * (MLA_HEADS * MLA_V) ** -0.5)
    ffn_w1 = nrm(ks[23], (DEPTH, D, D_FF), beta * D ** -0.5)
    ffn_w2 = nrm(ks[24], (DEPTH, D_FF, D), beta * D_FF ** -0.5)
    return {
        'x': x, 'c': c, 'pos_offset': pos_offset,
        'ada_w': ada_w, 'ada_b': ada_b, 'ln_g': ln_g, 'ln_b': ln_b,
        'rel_table': rel_table,
        'da_w_qkv': da_w_qkv, 'da_w_o': da_w_o,
        'da_lam_q1': da_lam_q1, 'da_lam_k1': da_lam_k1,
        'da_lam_q2': da_lam_q2, 'da_lam_k2': da_lam_k2, 'da_subln_w': da_subln_w,
        'mla_w_down': mla_w_down, 'mla_q_norm_w': mla_q_norm_w, 'mla_w_uq': mla_w_uq,
        'mla_kv_norm_w': mla_kv_norm_w, 'mla_w_ukv': mla_w_ukv, 'mla_w_o': mla_w_o,
        'ffn_w1': ffn_w1, 'ffn_w2': ffn_w2,
    }


def reference(x, c, pos_offset, ada_w, ada_b, ln_g, ln_b, rel_table,
              da_w_qkv, da_w_o, da_lam_q1, da_lam_k1, da_lam_q2, da_lam_k2, da_subln_w,
              mla_w_down, mla_q_norm_w, mla_w_uq, mla_kv_norm_w, mla_w_ukv, mla_w_o,
              ffn_w1, ffn_w2):
    B, S, _ = x.shape
    pos = pos_offset[:, None] + jnp.arange(S, dtype=jnp.int32)[None, :]
    c_act = jax.nn.silu(c)
    for i in range(DEPTH):
        mod = (c_act @ ada_w[i] + ada_b[i])[:, None, :]
        sh1, sc1, g1, sh2, sc2, g2 = jnp.split(mod, 6, axis=-1)
        u = x * (1.0 + sc1) + sh1
        j = i // N_MIXERS
        if i % N_MIXERS == 0:
            y = diff_attention(u, da_w_qkv[j], da_w_o[j], da_lam_q1[j], da_lam_k1[j],
                               da_lam_q2[j], da_lam_k2[j], da_subln_w[j], rel_table, i)
        else:
            y = mla_attention(u, pos, mla_w_down[j], mla_q_norm_w[j], mla_w_uq[j],
                              mla_kv_norm_w[j], mla_w_ukv[j], mla_w_o[j])
        x = _layer_norm(DEEPNORM_ALPHA * x + (1.0 + g1) * y, ln_g[i, 0], ln_b[i, 0])
        u = x * (1.0 + sc2) + sh2
        y = squared_relu_mlp(u, ffn_w1[i], ffn_w2[i])
        x = _layer_norm(DEEPNORM_ALPHA * x + (1.0 + g2) * y, ln_g[i, 1], ln_b[i, 1])
    return x
```

```python
import functools
import math

import jax
import jax.numpy as jnp
from jax import lax
from jax.experimental import pallas as pl
from jax.experimental.pallas import tpu as pltpu

F32 = jnp.float32
BF16 = jnp.bfloat16

D_MODEL = 1024
DEPTH = 2
CHUNK = 64
DA_HEAD_DIM = 64
DA_HEADS = D_MODEL // (2 * DA_HEAD_DIM)
MLA_HEADS = 16
MLA_NOPE = 64
MLA_ROPE = 32
MLA_V = 64
MLA_Q_LORA = 512
MLA_KV_LORA = 256
ROPE_THETA = 10000.0
REL_BUCKETS = 32
REL_MAX_DIST = 128
D_FF = 4 * D_MODEL
DEEPNORM_ALPHA = (2 * DEPTH) ** 0.25
LN_EPS = 1e-5
RMS_EPS = 1e-6

LANES = 128
HEAD_LANES = 128
ATTN_TILE = 256
ROW_TILE = 512
FF_CHUNK = 1024
LOG2E = math.log2(math.e)
NEG_BIG = -1e30
VMEM_LIMIT = 56 * 1024 * 1024

_NT = (((1,), (1,)), ((), ()))


def _dot(a, b):
    return jnp.dot(a, b, preferred_element_type=F32)


def _dot_nt(a, b):
    return lax.dot_general(a, b, _NT, preferred_element_type=F32)


def _layer_norm_rows(z, g, b):
    mu = jnp.mean(z, axis=-1, keepdims=True)
    zc = z - mu
    var = jnp.mean(zc * zc, axis=-1, keepdims=True)
    return zc * lax.rsqrt(var + LN_EPS) * g + b


def _rms_norm_rows(z, w):
    return z * lax.rsqrt(jnp.mean(z * z, axis=-1, keepdims=True) + RMS_EPS) * w


def _mod_kernel(c_ref, w_ref, b_ref, o_ref):
    c = c_ref[...]
    c_act = c * jax.nn.sigmoid(c)
    o_ref[0, 0] = _dot(c_act.astype(BF16), w_ref[0].astype(BF16)) + b_ref[0, 0]


def _modulation(c, ada_w, ada_b):
    b = c.shape[0]
    rows = 8
    c_pad = jnp.zeros((rows, D_MODEL), F32).at[:b].set(c)
    out = pl.pallas_call(
        _mod_kernel,
        grid=(DEPTH, 6),
        in_specs=[
            pl.BlockSpec((rows, D_MODEL), lambda i, j: (0, 0)),
            pl.BlockSpec((1, D_MODEL, D_MODEL), lambda i, j: (i, 0, j)),
            pl.BlockSpec((1, 1, 1, D_MODEL), lambda i, j: (i, j, 0, 0)),
        ],
        out_specs=pl.BlockSpec((1, 1, rows, D_MODEL), lambda i, j: (i, j, 0, 0)),
        out_shape=jax.ShapeDtypeStruct((DEPTH, 6, rows, D_MODEL), F32),
        compiler_params=pltpu.CompilerParams(vmem_limit_bytes=VMEM_LIMIT),
        name="mod",
    )(c_pad, ada_w, ada_b.reshape(DEPTH, 6, 1, D_MODEL))
    return out[:, :, :b].reshape(DEPTH, 6, b, 1, D_MODEL)


def _mod_spec(layer, idx):
    return pl.BlockSpec((None, None, None, 1, D_MODEL), lambda b, i: (layer, idx, b, 0, 0))


def _row_spec(width):
    return pl.BlockSpec((1, ROW_TILE, width), lambda b, i: (b, i, 0))


def _whole_spec(shape):
    return pl.BlockSpec(shape, lambda b, i: (0,) * len(shape))


def _vt_out_spec():
    return pl.BlockSpec((1, DA_HEADS, ROW_TILE // ATTN_TILE, HEAD_LANES, ATTN_TILE),
                        lambda b, i: (b, 0, i, 0, 0))


def _store_vt(vt_ref, vt):
    vt = vt.astype(BF16).reshape(DA_HEADS, HEAD_LANES, ROW_TILE)
    for t in range(ROW_TILE // ATTN_TILE):
        vt_ref[0, :, t] = vt[:, :, t * ATTN_TILE:(t + 1) * ATTN_TILE]


def _pre0_kernel(x_ref, sc_ref, sh_ref, wq_ref, wk_ref, wvt_ref, q_ref, k_ref, vt_ref, *, q_scale):
    u = (x_ref[0] * (1.0 + sc_ref[...]) + sh_ref[...]).astype(BF16)
    q_ref[0] = (_dot(u, wq_ref[...]) * q_scale).astype(BF16)
    k_ref[0] = _dot(u, wk_ref[...]).astype(BF16)
    _store_vt(vt_ref, _dot_nt(wvt_ref[...], u))


def _pre0(x, mod, wq, wk, wvt):
    b, s, d = x.shape
    q_scale = DA_HEAD_DIM ** -0.5 * LOG2E
    return pl.pallas_call(
        functools.partial(_pre0_kernel, q_scale=q_scale),
        grid=(b, s // ROW_TILE),
        in_specs=[_row_spec(d), _mod_spec(0, 1), _mod_spec(0, 0),
                  _whole_spec((d, d)), _whole_spec((d, d)), _whole_spec((d, d))],
        out_specs=[_row_spec(d), _row_spec(d), _vt_out_spec()],
        out_shape=[jax.ShapeDtypeStruct((b, s, d), BF16),
                   jax.ShapeDtypeStruct((b, s, d), BF16),
                   jax.ShapeDtypeStruct((b, DA_HEADS, s // ATTN_TILE, HEAD_LANES, ATTN_TILE), BF16)],
        compiler_params=pltpu.CompilerParams(vmem_limit_bytes=VMEM_LIMIT),
        name="pre0",
    )(x, mod, mod, wq, wk, wvt)


def _pre1_kernel(x_ref, sc_ref, sh_ref, wd_ref, qn_ref, kvn_ref, wuq_ref, wuk_ref, wuvt_ref,
                 cos_ref, sin_ref, q_ref, k_ref, vt_ref, *, q_scale):
    u = (x_ref[0] * (1.0 + sc_ref[...]) + sh_ref[...]).astype(BF16)
    down = _dot(u, wd_ref[...])
    c_q = _rms_norm_rows(down[:, :MLA_Q_LORA], qn_ref[...]).astype(BF16)
    c_kv = _rms_norm_rows(down[:, MLA_Q_LORA:MLA_Q_LORA + MLA_KV_LORA], kvn_ref[...]).astype(BF16)
    k_rope = down[:, MLA_Q_LORA + MLA_KV_LORA:]
    q = _dot(c_q, wuq_ref[...])
    k_nope = _dot(c_kv, wuk_ref[...])
    _store_vt(vt_ref, _dot_nt(wuvt_ref[...], c_kv))

    cos = cos_ref[0]
    sin = sin_ref[0]
    lane = lax.broadcasted_iota(jnp.int32, cos.shape, 1)
    first_half = lane < MLA_NOPE + MLA_ROPE // 2

    def rope(xh):
        partner = jnp.where(first_half,
                            pltpu.roll(xh, HEAD_LANES - MLA_ROPE // 2, 1),
                            pltpu.roll(xh, MLA_ROPE // 2, 1))
        return xh * cos + partner * sin

    k_rope = rope(k_rope)
    for h in range(MLA_HEADS):
        sl = slice(h * HEAD_LANES, (h + 1) * HEAD_LANES)
        q_ref[0, :, sl] = (rope(q[:, sl]) * q_scale).astype(BF16)
        k_ref[0, :, sl] = (k_nope[:, sl] + k_rope).astype(BF16)


def _pre1(x, mod, wd, qn, kvn, wuq, wuk, wuvt, cos_t, sin_t):
    b, s, d = x.shape
    hw = MLA_HEADS * HEAD_LANES
    q_scale = (MLA_NOPE + MLA_ROPE) ** -0.5 * LOG2E
    return pl.pallas_call(
        functools.partial(_pre1_kernel, q_scale=q_scale),
        grid=(b, s // ROW_TILE),
        in_specs=[_row_spec(d), _mod_spec(1, 1), _mod_spec(1, 0),
                  _whole_spec(wd.shape), _whole_spec(qn.shape), _whole_spec(kvn.shape),
                  _whole_spec(wuq.shape), _whole_spec(wuk.shape), _whole_spec(wuvt.shape),
                  _row_spec(HEAD_LANES), _row_spec(HEAD_LANES)],
        out_specs=[_row_spec(hw), _row_spec(hw), _vt_out_spec()],
        out_shape=[jax.ShapeDtypeStruct((b, s, hw), BF16),
                   jax.ShapeDtypeStruct((b, s, hw), BF16),
                   jax.ShapeDtypeStruct((b, DA_HEADS, s // ATTN_TILE, HEAD_LANES, ATTN_TILE), BF16)],
        compiler_params=pltpu.CompilerParams(vmem_limit_bytes=VMEM_LIMIT),
        name="pre1",
    )(x, mod, mod, wd, qn, kvn, wuq, wuk, wuvt, cos_t, sin_t)


def _t5_bucket(rel):
    nb = REL_BUCKETS // 2
    ret = (rel > 0).astype(jnp.int32) * nb
    n = jnp.abs(rel)
    max_exact = nb // 2
    is_small = n < max_exact
    n_f = jnp.maximum(n, 1).astype(F32)
    large = max_exact + (jnp.log(n_f / max_exact) / math.log(REL_MAX_DIST / max_exact)
                         * (nb - max_exact)).astype(jnp.int32)
    large = jnp.minimum(large, nb - 1)
    return ret + jnp.where(is_small, n, large)


def _near_bucket_tiles():
    kk = jnp.arange(ATTN_TILE, dtype=jnp.int32)[:, None]
    qq = jnp.arange(ATTN_TILE, dtype=jnp.int32)[None, :]
    diag = jnp.where(kk // CHUNK <= qq // CHUNK, _t5_bucket(kk - qq), -1)
    prev = _t5_bucket(kk - ATTN_TILE - qq)
    return jnp.stack([diag, prev])


def _bias_kernel(tab_ref, idx_ref, o_ref):
    h = pl.program_id(0)
    idx = idx_ref[...]
    val = jnp.zeros(idx.shape, F32)
    for bkt in range(REL_BUCKETS):
        val = jnp.where(idx == bkt, tab_ref[bkt, h], val)
    far = tab_ref[REL_BUCKETS // 2 - 1, h]
    o_ref[0] = jnp.where(idx < 0, NEG_BIG, (val - far) * LOG2E)


def _near_bias(rel_table):
    idx = _near_bucket_tiles()
    return pl.pallas_call(
        _bias_kernel,
        grid=(DA_HEADS,),
        in_specs=[pl.BlockSpec(memory_space=pltpu.SMEM),
                  pl.BlockSpec(idx.shape, lambda h: (0, 0, 0))],
        out_specs=pl.BlockSpec((1,) + idx.shape, lambda h: (h, 0, 0, 0)),
        out_shape=jax.ShapeDtypeStruct((DA_HEADS,) + idx.shape, F32),
        name="near_bias",
    )(rel_table, idx)


def _near_mask():
    kk = jnp.arange(ATTN_TILE, dtype=jnp.int32)[:, None]
    qq = jnp.arange(ATTN_TILE, dtype=jnp.int32)[None, :]
    return jnp.where(kk // CHUNK <= qq // CHUNK, 0.0, NEG_BIG).astype(F32)[None, None]


def _attn_kernel(*refs, diff, lam_init):
    if diff:
        q_ref, k_ref, vt_ref, near_ref, lam_ref, subw_ref, o_ref, m_ref, l_ref, acc_ref = refs
    else:
        q_ref, k_ref, vt_ref, near_ref, o_ref, m_ref, l_ref, acc_ref = refs
    t = ATTN_TILE
    i = pl.program_id(2)
    n_near = near_ref.shape[1]
    dv = acc_ref.shape[1]

    q = q_ref[0]
    if diff:
        lane = lax.broadcasted_iota(jnp.int32, q.shape, 1)
        zero = jnp.zeros_like(q)
        q_maps = (jnp.where(lane < DA_HEAD_DIM, q, zero), jnp.where(lane >= DA_HEAD_DIM, q, zero))
    else:
        q_maps = (q[:, :HEAD_LANES], q[:, HEAD_LANES:])

    m_ref[...] = jnp.full(m_ref.shape, NEG_BIG, F32)
    l_ref[...] = jnp.zeros(l_ref.shape, F32)
    acc_ref[...] = jnp.zeros(acc_ref.shape, F32)

    def step(j, near_idx):
        k = k_ref[0, pl.ds(pl.multiple_of(j * t, t), t), :]
        vt = vt_ref[0, 0, j]
        for mp in range(2):
            k_m = k if diff else k[:, mp * HEAD_LANES:(mp + 1) * HEAD_LANES]
            vt_m = vt if diff else vt[mp * dv:(mp + 1) * dv]
            s = _dot_nt(k_m, q_maps[mp])
            if near_idx is not None:
                s = s + near_ref[0, near_idx]
            m_old = m_ref[mp]
            m_new = jnp.maximum(m_old, jnp.max(s, axis=0, keepdims=True))
            alpha = jnp.exp2(m_old - m_new)
            p = jnp.exp2(s - m_new)
            l_ref[mp] = alpha * l_ref[mp] + jnp.sum(p, axis=0, keepdims=True)
            acc_ref[mp] = alpha * acc_ref[mp] + _dot(vt_m, p.astype(BF16))
            m_ref[mp] = m_new

    def far_body(j, carry):
        step(j, None)
        return carry

    lax.fori_loop(0, jnp.maximum(i - (n_near - 1), 0), far_body, 0)
    if n_near == 2:
        @pl.when(i >= 1)
        def _():
            step(i - 1, 1)
    step(i, 0)

    o_a = acc_ref[0] / l_ref[0]
    o_b = acc_ref[1] / l_ref[1]
    if diff:
        lam = (jnp.exp(jnp.sum(lam_ref[0:1] * lam_ref[1:2], axis=-1, keepdims=True))
               - jnp.exp(jnp.sum(lam_ref[2:3] * lam_ref[3:4], axis=-1, keepdims=True)) + lam_init)
        o = o_a - lam * o_b
        o = o * lax.rsqrt(jnp.mean(o * o, axis=0, keepdims=True) + RMS_EPS)
        o = o * (subw_ref[...] * (1.0 - lam_init))
    else:
        o = jnp.concatenate([o_a, o_b], axis=0)
    o_ref[0] = o.T.astype(BF16)


def _attention(q, k, vt, near, lam_vecs=None, subln_w=None, *, diff, lam_init=0.0):
    b, s, _ = q.shape
    t = ATTN_TILE
    heads = DA_HEADS
    width = HEAD_LANES if diff else 2 * HEAD_LANES
    dv = HEAD_LANES if diff else MLA_V
    near_heads = near.shape[0]
    in_specs = [
        pl.BlockSpec((1, t, width), lambda bi, h, i: (bi, i, h)),
        pl.BlockSpec((1, s, width), lambda bi, h, i: (bi, 0, h)),
        pl.BlockSpec((1, 1, s // t, HEAD_LANES, t), lambda bi, h, i: (bi, h, 0, 0, 0)),
        pl.BlockSpec((1,) + near.shape[1:], lambda bi, h, i: (h if near_heads > 1 else 0, 0, 0, 0)),
    ]
    args = [q, k, vt, near]
    if diff:
        in_specs += [pl.BlockSpec(lam_vecs.shape, lambda bi, h, i: (0, 0)),
                     pl.BlockSpec(subln_w.shape, lambda bi, h, i: (0, 0))]
        args += [lam_vecs, subln_w]
    return pl.pallas_call(
        functools.partial(_attn_kernel, diff=diff, lam_init=lam_init),
        grid=(b, heads, s // t),
        in_specs=in_specs,
        out_specs=pl.BlockSpec((1, t, HEAD_LANES), lambda bi, h, i: (bi, i, h)),
        out_shape=jax.ShapeDtypeStruct((b, s, heads * HEAD_LANES), BF16),
        scratch_shapes=[pltpu.VMEM((2, 1, t), F32), pltpu.VMEM((2, 1, t), F32),
                        pltpu.VMEM((2, dv, t), F32)],
        compiler_params=pltpu.CompilerParams(vmem_limit_bytes=VMEM_LIMIT),
        name="attn_diff" if diff else "attn_mla",
    )(*args)


def _post_kernel(o_ref, x_ref, g_ref, wo_ref, lng_ref, lnb_ref, sc_ref, sh_ref, x1_ref, u_ref):
    y = _dot(o_ref[0], wo_ref[...])
    z = DEEPNORM_ALPHA * x_ref[0] + (1.0 + g_ref[...]) * y
    xn = _layer_norm_rows(z, lng_ref[...], lnb_ref[...])
    x1_ref[0] = xn
    u_ref[0] = (xn * (1.0 + sc_ref[...]) + sh_ref[...]).astype(BF16)


def _post(o, x, mod, layer, wo, ln_g, ln_b):
    b, s, d = x.shape
    return pl.pallas_call(
        _post_kernel,
        grid=(b, s // ROW_TILE),
        in_specs=[_row_spec(d), _row_spec(d), _mod_spec(layer, 2), _whole_spec((d, d)),
                  _whole_spec((1, d)), _whole_spec((1, d)), _mod_spec(layer, 4), _mod_spec(layer, 3)],
        out_specs=[_row_spec(d), _row_spec(d)],
        out_shape=[jax.ShapeDtypeStruct((b, s, d), F32), jax.ShapeDtypeStruct((b, s, d), BF16)],
        compiler_params=pltpu.CompilerParams(vmem_limit_bytes=VMEM_LIMIT),
        name=f"post{layer}",
    )(o, x, mod, wo, ln_g, ln_b, mod, mod)


def _ffn_kernel(u_ref, x_ref, g_ref, w1_ref, w2_ref, lng_ref, lnb_ref, o_ref):
    u = u_ref[0]
    y = jnp.zeros(x_ref.shape[1:], F32)
    for c in range(D_FF // FF_CHUNK):
        sl = slice(c * FF_CHUNK, (c + 1) * FF_CHUNK)
        h = jnp.maximum(_dot(u, w1_ref[:, sl]), 0.0)
        y = y + _dot((h * h).astype(BF16), w2_ref[sl, :])
    z = DEEPNORM_ALPHA * x_ref[0] + (1.0 + g_ref[...]) * y
    o_ref[0] = _layer_norm_rows(z, lng_ref[...], lnb_ref[...])


def _ffn(u, x, mod, layer, w1, w2, ln_g, ln_b):
    b, s, d = x.shape
    return pl.pallas_call(
        _ffn_kernel,
        grid=(b, s // ROW_TILE),
        in_specs=[_row_spec(d), _row_spec(d), _mod_spec(layer, 5),
                  _whole_spec((d, D_FF)), _whole_spec((D_FF, d)),
                  _whole_spec((1, d)), _whole_spec((1, d))],
        out_specs=_row_spec(d),
        out_shape=jax.ShapeDtypeStruct((b, s, d), F32),
        compiler_params=pltpu.CompilerParams(vmem_limit_bytes=VMEM_LIMIT),
        name=f"ffn{layer}",
    )(u, x, mod, w1, w2, ln_g, ln_b)


def _rope_tables(pos_offset, s):
    half = MLA_ROPE // 2
    pos = pos_offset[:, None] + jnp.arange(s, dtype=jnp.int32)[None, :]
    inv = ROPE_THETA ** (-jnp.arange(half, dtype=F32) / half)
    ang = pos.astype(F32)[..., None] * inv
    cos = jnp.cos(ang)
    sin = jnp.sin(ang)
    b = pos.shape[0]
    ones = jnp.ones((b, s, MLA_NOPE), F32)
    zeros_tail = jnp.zeros((b, s, HEAD_LANES - MLA_NOPE - MLA_ROPE), F32)
    cos_t = jnp.concatenate([ones, cos, cos, zeros_tail], axis=-1)
    sin_t = jnp.concatenate([jnp.zeros_like(ones), -sin, sin, zeros_tail], axis=-1)
    return cos_t, sin_t


def _pad_heads(w, per_head, lead):
    w = w.reshape(lead, MLA_HEADS, per_head)
    w = jnp.pad(w, ((0, 0), (0, 0), (0, HEAD_LANES - per_head)))
    return w.reshape(lead, MLA_HEADS * HEAD_LANES)


def kernel(x, c, pos_offset, ada_w, ada_b, ln_g, ln_b, rel_table, da_w_qkv, da_w_o, da_lam_q1, da_lam_k1, da_lam_q2, da_lam_k2, da_subln_w, mla_w_down, mla_q_norm_w, mla_w_uq, mla_kv_norm_w, mla_w_ukv, mla_w_o, ffn_w1, ffn_w2):
    b, s, d = x.shape
    assert d == D_MODEL and s % ROW_TILE == 0 and ROW_TILE % ATTN_TILE == 0 and ATTN_TILE % CHUNK == 0

    mod = _modulation(c, ada_w, ada_b)

    w_qkv = da_w_qkv[0].astype(BF16)
    q, k, vt = _pre0(x, mod, w_qkv[:, :d], w_qkv[:, d:2 * d], w_qkv[:, 2 * d:].T)
    lam_vecs = jnp.stack([da_lam_q1[0], da_lam_k1[0], da_lam_q2[0], da_lam_k2[0]])
    lam_init = 0.8 - 0.6 * math.exp(-0.3 * 0)
    o = _attention(q, k, vt, _near_bias(rel_table), lam_vecs, da_subln_w[0].reshape(-1, 1),
                   diff=True, lam_init=lam_init)
    x, u = _post(o, x, mod, 0, da_w_o[0].astype(BF16), ln_g[0, 0][None], ln_b[0, 0][None])
    x = _ffn(u, x, mod, 0, ffn_w1[0].astype(BF16), ffn_w2[0].astype(BF16), ln_g[0, 1][None], ln_b[0, 1][None])

    n_lat = MLA_Q_LORA + MLA_KV_LORA
    w_down = mla_w_down[0]
    w_down = jnp.concatenate(
        [w_down[:, :n_lat], jnp.zeros((d, MLA_NOPE), F32), w_down[:, n_lat:],
         jnp.zeros((d, HEAD_LANES - MLA_NOPE - MLA_ROPE), F32)], axis=-1).astype(BF16)
    w_uq = _pad_heads(mla_w_uq[0], MLA_NOPE + MLA_ROPE, MLA_Q_LORA).astype(BF16)
    w_ukv = mla_w_ukv[0].reshape(MLA_KV_LORA, MLA_HEADS, MLA_NOPE + MLA_V)
    w_uk = _pad_heads(w_ukv[..., :MLA_NOPE].reshape(MLA_KV_LORA, -1), MLA_NOPE, MLA_KV_LORA).astype(BF16)
    w_uvt = w_ukv[..., MLA_NOPE:].reshape(MLA_KV_LORA, -1).T.astype(BF16)
    cos_t, sin_t = _rope_tables(pos_offset, s)
    q, k, vt = _pre1(x, mod, w_down, mla_q_norm_w[0][None], mla_kv_norm_w[0][None],
                     w_uq, w_uk, w_uvt, cos_t, sin_t)
    o = _attention(q, k, vt, _near_mask(), diff=False)
    x, u = _post(o, x, mod, 1, mla_w_o[0].astype(BF16), ln_g[1, 0][None], ln_b[1, 0][None])
    x = _ffn(u, x, mod, 1, ffn_w1[1].astype(BF16), ffn_w2[1].astype(BF16), ln_g[1, 1][None], ln_b[1, 1][None])
    return x
```

```python
import functools
import math

import jax
import jax.numpy as jnp
import numpy as np
from jax import lax
from jax.experimental import pallas as pl
from jax.experimental.pallas import tpu as pltpu

F32 = jnp.float32
BF16 = jnp.bfloat16

D_MODEL = 1024
DEPTH = 2
CHUNK = 64
DA_HEAD_DIM = 64
DA_HEADS = D_MODEL // (2 * DA_HEAD_DIM)
MLA_HEADS = 16
MLA_NOPE = 64
MLA_ROPE = 32
MLA_V = 64
MLA_Q_LORA = 512
MLA_KV_LORA = 256
ROPE_THETA = 10000.0
REL_BUCKETS = 32
REL_MAX_DIST = 128
D_FF = 4 * D_MODEL
DEEPNORM_ALPHA = (2 * DEPTH) ** 0.25
LN_EPS = 1e-5
RMS_EPS = 1e-6

LANES = 128
HEAD_LANES = 128
ATTN_TILE = 256
ROW_TILE = 512
FF_CHUNK = 1024
LOG2E = math.log2(math.e)
NEG_BIG = -1e30
VMEM_LIMIT = 56 * 1024 * 1024

_NT = (((1,), (1,)), ((), ()))


def _dot(a, b):
    return jnp.dot(a, b, preferred_element_type=F32)


def _dot_nt(a, b):
    return lax.dot_general(a, b, _NT, preferred_element_type=F32)


def _layer_norm_rows(z, g, b):
    mu = jnp.mean(z, axis=-1, keepdims=True)
    zc = z - mu
    var = jnp.mean(zc * zc, axis=-1, keepdims=True)
    return zc * lax.rsqrt(var + LN_EPS) * g + b


def _rms_norm_rows(z, w):
    return z * lax.rsqrt(jnp.mean(z * z, axis=-1, keepdims=True) + RMS_EPS) * w


def _mod_kernel(c_ref, w_ref, b_ref, o_ref):
    c = c_ref[...]
    c_act = c * jax.nn.sigmoid(c)
    o_ref[0, 0] = _dot(c_act.astype(BF16), w_ref[0].astype(BF16)) + b_ref[0, 0]


def _modulation(c, ada_w, ada_b):
    b = c.shape[0]
    rows = 8
    c_pad = jnp.zeros((rows, D_MODEL), F32).at[:b].set(c)
    out = pl.pallas_call(
        _mod_kernel,
        grid=(DEPTH, 6),
        in_specs=[
            pl.BlockSpec((rows, D_MODEL), lambda i, j: (0, 0)),
            pl.BlockSpec((1, D_MODEL, D_MODEL), lambda i, j: (i, 0, j)),
            pl.BlockSpec((1, 1, 1, D_MODEL), lambda i, j: (i, j, 0, 0)),
        ],
        out_specs=pl.BlockSpec((1, 1, rows, D_MODEL), lambda i, j: (i, j, 0, 0)),
        out_shape=jax.ShapeDtypeStruct((DEPTH, 6, rows, D_MODEL), F32),
        compiler_params=pltpu.CompilerParams(vmem_limit_bytes=VMEM_LIMIT),
        name="mod",
    )(c_pad, ada_w, ada_b.reshape(DEPTH, 6, 1, D_MODEL))
    return out[:, :, :b].reshape(DEPTH, 6, b, 1, D_MODEL)


def _mod_spec(layer, idx):
    return pl.BlockSpec((None, None, None, 1, D_MODEL), lambda b, i: (layer, idx, b, 0, 0))


def _row_spec(width):
    return pl.BlockSpec((1, ROW_TILE, width), lambda b, i: (b, i, 0))


def _whole_spec(shape):
    return pl.BlockSpec(shape, lambda b, i: (0,) * len(shape))


def _vt_out_spec():
    return pl.BlockSpec((1, DA_HEADS, ROW_TILE // ATTN_TILE, HEAD_LANES, ATTN_TILE),
                        lambda b, i: (b, 0, i, 0, 0))


def _store_vt(vt_ref, vt):
    vt = vt.astype(BF16).reshape(DA_HEADS, HEAD_LANES, ROW_TILE)
    for t in range(ROW_TILE // ATTN_TILE):
        vt_ref[0, :, t] = vt[:, :, t * ATTN_TILE:(t + 1) * ATTN_TILE]


def _pre0_kernel(x_ref, sc_ref, sh_ref, wq_ref, wk_ref, wvt_ref, q_ref, k_ref, vt_ref, *, q_scale):
    u = (x_ref[0] * (1.0 + sc_ref[...]) + sh_ref[...]).astype(BF16)
    q_ref[0] = (_dot(u, wq_ref[...]) * q_scale).astype(BF16)
    k_ref[0] = _dot(u, wk_ref[...]).astype(BF16)
    _store_vt(vt_ref, _dot_nt(wvt_ref[...], u))


def _pre0(x, mod, wq, wk, wvt):
    b, s, d = x.shape
    q_scale = DA_HEAD_DIM ** -0.5 * LOG2E
    return pl.pallas_call(
        functools.partial(_pre0_kernel, q_scale=q_scale),
        grid=(b, s // ROW_TILE),
        in_specs=[_row_spec(d), _mod_spec(0, 1), _mod_spec(0, 0),
                  _whole_spec((d, d)), _whole_spec((d, d)), _whole_spec((d, d))],
        out_specs=[_row_spec(d), _row_spec(d), _vt_out_spec()],
        out_shape=[jax.ShapeDtypeStruct((b, s, d), BF16),
                   jax.ShapeDtypeStruct((b, s, d), BF16),
                   jax.ShapeDtypeStruct((b, DA_HEADS, s // ATTN_TILE, HEAD_LANES, ATTN_TILE), BF16)],
        compiler_params=pltpu.CompilerParams(vmem_limit_bytes=VMEM_LIMIT),
        name="pre0",
    )(x, mod, mod, wq, wk, wvt)


def _pre1_kernel(x_ref, sc_ref, sh_ref, wd_ref, qn_ref, kvn_ref, wuq_ref, wuk_ref, wuvt_ref,
                 cos_ref, sin_ref, q_ref, k_ref, vt_ref, *, q_scale):
    u = (x_ref[0] * (1.0 + sc_ref[...]) + sh_ref[...]).astype(BF16)
    down = _dot(u, wd_ref[...])
    c_q = _rms_norm_rows(down[:, :MLA_Q_LORA], qn_ref[...]).astype(BF16)
    c_kv = _rms_norm_rows(down[:, MLA_Q_LORA:MLA_Q_LORA + MLA_KV_LORA], kvn_ref[...]).astype(BF16)
    k_rope = down[:, MLA_Q_LORA + MLA_KV_LORA:]
    q = _dot(c_q, wuq_ref[...])
    k_nope = _dot(c_kv, wuk_ref[...])
    _store_vt(vt_ref, _dot_nt(wuvt_ref[...], c_kv))

    cos = cos_ref[0]
    sin = sin_ref[0]
    lane = lax.broadcasted_iota(jnp.int32, cos.shape, 1)
    first_half = lane < MLA_NOPE + MLA_ROPE // 2

    def rope(xh):
        partner = jnp.where(first_half,
                            pltpu.roll(xh, HEAD_LANES - MLA_ROPE // 2, 1),
                            pltpu.roll(xh, MLA_ROPE // 2, 1))
        return xh * cos + partner * sin

    k_rope = rope(k_rope)
    for h in range(MLA_HEADS):
        sl = slice(h * HEAD_LANES, (h + 1) * HEAD_LANES)
        q_ref[0, :, sl] = (rope(q[:, sl]) * q_scale).astype(BF16)
        k_ref[0, :, sl] = (k_nope[:, sl] + k_rope).astype(BF16)


def _pre1(x, mod, wd, qn, kvn, wuq, wuk, wuvt, cos_t, sin_t):
    b, s, d = x.shape
    hw = MLA_HEADS * HEAD_LANES
    q_scale = (MLA_NOPE + MLA_ROPE) ** -0.5 * LOG2E
    return pl.pallas_call(
        functools.partial(_pre1_kernel, q_scale=q_scale),
        grid=(b, s // ROW_TILE),
        in_specs=[_row_spec(d), _mod_spec(1, 1), _mod_spec(1, 0),
                  _whole_spec(wd.shape), _whole_spec(qn.shape), _whole_spec(kvn.shape),
                  _whole_spec(wuq.shape), _whole_spec(wuk.shape), _whole_spec(wuvt.shape),
                  _row_spec(HEAD_LANES), _row_spec(HEAD_LANES)],
        out_specs=[_row_spec(hw), _row_spec(hw), _vt_out_spec()],
        out_shape=[jax.ShapeDtypeStruct((b, s, hw), BF16),
                   jax.ShapeDtypeStruct((b, s, hw), BF16),
                   jax.ShapeDtypeStruct((b, DA_HEADS, s // ATTN_TILE, HEAD_LANES, ATTN_TILE), BF16)],
        compiler_params=pltpu.CompilerParams(vmem_limit_bytes=VMEM_LIMIT),
        name="pre1",
    )(x, mod, mod, wd, qn, kvn, wuq, wuk, wuvt, cos_t, sin_t)


def _t5_bucket(rel):
    nb = REL_BUCKETS // 2
    ret = (rel > 0).astype(jnp.int32) * nb
    n = jnp.abs(rel)
    max_exact = nb // 2
    is_small = n < max_exact
    n_f = jnp.maximum(n, 1).astype(F32)
    large = max_exact + (jnp.log(n_f / max_exact) / math.log(REL_MAX_DIST / max_exact)
                         * (nb - max_exact)).astype(jnp.int32)
    large = jnp.minimum(large, nb - 1)
    return ret + jnp.where(is_small, n, large)


def _near_bucket_tiles():
    kk = jnp.arange(ATTN_TILE, dtype=jnp.int32)[:, None]
    qq = jnp.arange(ATTN_TILE, dtype=jnp.int32)[None, :]
    diag = jnp.where(kk // CHUNK <= qq // CHUNK, _t5_bucket(kk - qq), -1)
    prev = _t5_bucket(kk - ATTN_TILE - qq)
    return jnp.stack([diag, prev])


def _bias_kernel(tab_ref, idx_ref, o_ref):
    h = pl.program_id(0)
    idx = idx_ref[...]
    val = jnp.zeros(idx.shape, F32)
    for bkt in range(REL_BUCKETS):
        val = jnp.where(idx == bkt, tab_ref[bkt, h], val)
    far = tab_ref[REL_BUCKETS // 2 - 1, h]
    o_ref[0] = jnp.where(idx < 0, NEG_BIG, (val - far) * LOG2E)


def _near_bias(rel_table):
    idx = _near_bucket_tiles()
    return pl.pallas_call(
        _bias_kernel,
        grid=(DA_HEADS,),
        in_specs=[pl.BlockSpec(memory_space=pltpu.SMEM),
                  pl.BlockSpec(idx.shape, lambda h: (0, 0, 0))],
        out_specs=pl.BlockSpec((1,) + idx.shape, lambda h: (h, 0, 0, 0)),
        out_shape=jax.ShapeDtypeStruct((DA_HEADS,) + idx.shape, F32),
        name="near_bias",
    )(rel_table, idx)


def _near_mask():
    kk = jnp.arange(ATTN_TILE, dtype=jnp.int32)[:, None]
    qq = jnp.arange(ATTN_TILE, dtype=jnp.int32)[None, :]
    return jnp.where(kk // CHUNK <= qq // CHUNK, 0.0, NEG_BIG).astype(F32)[None, None]


def _tile_tables(nq, n_near_kinds):
    ti, tj, ta = [], [], []
    for kind in range(n_near_kinds):
        for i in range(kind, nq):
            ti.append(i), tj.append(i - kind), ta.append(kind)
    n_near = len(ti)
    for j in range(nq - n_near_kinds):
        for i in range(j + n_near_kinds, nq):
            ti.append(i), tj.append(j), ta.append(0)
    return np.asarray(ti, np.int32), np.asarray(tj, np.int32), np.asarray(ta, np.int32), n_near


def _attn_kernel(*refs, diff, lam_init, n_near, n_tiles):
    ti_ref, tj_ref, ta_ref, q_ref, k_ref, vt_ref, near_ref = refs[:7]
    refs = refs[7:]
    if diff:
        lam_ref, subw_ref = refs[:2]
        refs = refs[2:]
    o_ref, s0, s1, p0, p1, mx0, mx1, al0, al1, m_scr, l_scr, acc = refs[:12]
    s_buf, p_buf, mx_buf, al_buf = (s0, s1), (p0, p1), (mx0, mx1), (al0, al1)
    t = ATTN_TILE
    nq = acc.shape[1]
    dv = acc.shape[2]

    if diff:
        qm = refs[12]
        q_all = q_ref[0]
        lane = lax.broadcasted_iota(jnp.int32, q_all.shape, 1)
        zero = jnp.zeros_like(q_all)
        qm[0] = jnp.where(lane < DA_HEAD_DIM, q_all, zero)
        qm[1] = jnp.where(lane >= DA_HEAD_DIM, q_all, zero)

    m_scr[...] = jnp.full(m_scr.shape, NEG_BIG, F32)
    l_scr[...] = jnp.zeros(l_scr.shape, F32)
    acc[...] = jnp.zeros(acc.shape, F32)

    def rows(idx):
        return pl.ds(pl.multiple_of(idx * t, t), t)

    def q_map(mp, i):
        if diff:
            return qm[mp, rows(i), :]
        return q_ref[0, rows(i), mp * HEAD_LANES:(mp + 1) * HEAD_LANES]

    def score_dots(n):
        i, j = ti_ref[n], tj_ref[n]
        k = k_ref[0, rows(j), :]
        out = []
        for mp in range(2):
            k_m = k if diff else k[:, mp * HEAD_LANES:(mp + 1) * HEAD_LANES]
            out.append(_dot_nt(k_m, q_map(mp, i)))
        return out

    def score_store(n, slot, mp, s, near):
        if near:
            s = s + near_ref[0, ta_ref[n]]
        s_buf[slot][mp] = s
        mx_buf[slot][mp] = jnp.max(s, axis=0, keepdims=True)

    def exps(n, slot, mp):
        i = ti_ref[n]
        m_old = m_scr[mp, i]
        m_new = jnp.maximum(m_old, mx_buf[slot][mp])
        alpha = jnp.exp2(m_old - m_new)
        p = jnp.exp2(s_buf[slot][mp] - m_new)
        l_scr[mp, i] = alpha * l_scr[mp, i] + jnp.sum(p, axis=0, keepdims=True)
        m_scr[mp, i] = m_new
        p_buf[slot][mp] = p.astype(BF16)
        al_buf[slot][mp] = alpha

    def pv_dots(n, slot):
        vt = vt_ref[0, 0, tj_ref[n]]
        out = []
        for mp in range(2):
            vt_m = vt if diff else vt[mp * dv:(mp + 1) * dv]
            out.append(_dot(vt_m, p_buf[slot][mp]))
        return out

    def acc_update(n, slot, mp, pv):
        i = ti_ref[n]
        acc[mp, i] = al_buf[slot][mp] * acc[mp, i] + pv

    def iteration(n, parity, near, has1=True, has2=True, has3=True):
        pvs = pv_dots(n - 2, parity) if has3 else None
        ss = score_dots(n) if has1 else None
        for mp in range(2):
            if has2:
                exps(n - 1, 1 - parity, mp)
            if has1:
                score_store(n, parity, mp, ss[mp], near)
        if has3:
            for mp in range(2):
                acc_update(n - 2, parity, mp, pvs[mp])

    def run_range(lo, hi, near):
        if (hi - lo) % 2:
            iteration(lo, lo % 2, near)
            lo += 1

        def body(it, carry):
            n = lo + 2 * it
            iteration(n, lo % 2, near)
            iteration(n + 1, (lo + 1) % 2, near)
            return carry

        lax.fori_loop(0, (hi - lo) // 2, body, 0)

    iteration(0, 0, True, has2=False, has3=False)
    iteration(1, 1, True, has3=False)
    run_range(2, n_near, True)
    run_range(n_near, n_tiles, False)
    iteration(n_tiles, n_tiles % 2, False, has1=False)
    iteration(n_tiles + 1, (n_tiles + 1) % 2, False, has1=False, has2=False)

    if diff:
        lam = (jnp.exp(jnp.sum(lam_ref[0:1] * lam_ref[1:2], axis=-1, keepdims=True))
               - jnp.exp(jnp.sum(lam_ref[2:3] * lam_ref[3:4], axis=-1, keepdims=True)) + lam_init)
        sub_w = subw_ref[...] * (1.0 - lam_init)

    def finalize(i, carry):
        o_a = acc[0, i] / l_scr[0, i]
        o_b = acc[1, i] / l_scr[1, i]
        if diff:
            o = o_a - lam * o_b
            o = o * lax.rsqrt(jnp.mean(o * o, axis=0, keepdims=True) + RMS_EPS) * sub_w
        else:
            o = jnp.concatenate([o_a, o_b], axis=0)
        o_ref[0, rows(i), :] = o.T.astype(BF16)
        return carry

    lax.fori_loop(0, nq, finalize, 0)


def _attention(q, k, vt, near, lam_vecs=None, subln_w=None, *, diff, lam_init=0.0):
    b, s, _ = q.shape
    t = ATTN_TILE
    nq = s // t
    heads = DA_HEADS
    width = HEAD_LANES if diff else 2 * HEAD_LANES
    dv = HEAD_LANES if diff else MLA_V
    near_heads = near.shape[0]
    ti, tj, ta, n_near = _tile_tables(nq, near.shape[1])
    n_tiles = len(ti)
    assert n_near >= 2 and n_tiles > n_near
    smem = pl.BlockSpec(memory_space=pltpu.SMEM)
    in_specs = [
        smem, smem, smem,
        pl.BlockSpec((1, s, width), lambda bi, h: (bi, 0, h)),
        pl.BlockSpec((1, s, width), lambda bi, h: (bi, 0, h)),
        pl.BlockSpec((1, 1, nq, HEAD_LANES, t), lambda bi, h: (bi, h, 0, 0, 0)),
        pl.BlockSpec((1,) + near.shape[1:], lambda bi, h: (h if near_heads > 1 else 0, 0, 0, 0)),
    ]
    args = [jnp.asarray(ti), jnp.asarray(tj), jnp.asarray(ta), q, k, vt, near]
    if diff:
        in_specs += [pl.BlockSpec(lam_vecs.shape, lambda bi, h: (0, 0)),
                     pl.BlockSpec(subln_w.shape, lambda bi, h: (0, 0))]
        args += [lam_vecs, subln_w]
    tile_f32 = pltpu.VMEM((2, t, t), F32)
    tile_bf16 = pltpu.VMEM((2, t, t), BF16)
    row_f32 = pltpu.VMEM((2, 1, t), F32)
    scratch = [tile_f32, tile_f32, tile_bf16, tile_bf16, row_f32, row_f32, row_f32, row_f32,
               pltpu.VMEM((2, nq, 1, t), F32), pltpu.VMEM((2, nq, 1, t), F32),
               pltpu.VMEM((2, nq, dv, t), F32)]
    if diff:
        scratch.append(pltpu.VMEM((2, s, HEAD_LANES), BF16))
    return pl.pallas_call(
        functools.partial(_attn_kernel, diff=diff, lam_init=lam_init, n_near=n_near, n_tiles=n_tiles),
        grid=(b, heads),
        in_specs=in_specs,
        out_specs=pl.BlockSpec((1, s, HEAD_LANES), lambda bi, h: (bi, 0, h)),
        out_shape=jax.ShapeDtypeStruct((b, s, heads * HEAD_LANES), BF16),
        scratch_shapes=scratch,
        compiler_params=pltpu.CompilerParams(vmem_limit_bytes=VMEM_LIMIT),
        name="attn_diff" if diff else "attn_mla",
    )(*args)


def _post_kernel(o_ref, x_ref, g_ref, wo_ref, lng_ref, lnb_ref, sc_ref, sh_ref, x1_ref, u_ref):
    y = _dot(o_ref[0], wo_ref[...])
    z = DEEPNORM_ALPHA * x_ref[0] + (1.0 + g_ref[...]) * y
    xn = _layer_norm_rows(z, lng_ref[...], lnb_ref[...])
    x1_ref[0] = xn
    u_ref[0] = (xn * (1.0 + sc_ref[...]) + sh_ref[...]).astype(BF16)


def _post(o, x, mod, layer, wo, ln_g, ln_b):
    b, s, d = x.shape
    return pl.pallas_call(
        _post_kernel,
        grid=(b, s // ROW_TILE),
        in_specs=[_row_spec(d), _row_spec(d), _mod_spec(layer, 2), _whole_spec((d, d)),
                  _whole_spec((1, d)), _whole_spec((1, d)), _mod_spec(layer, 4), _mod_spec(layer, 3)],
        out_specs=[_row_spec(d), _row_spec(d)],
        out_shape=[jax.ShapeDtypeStruct((b, s, d), F32), jax.ShapeDtypeStruct((b, s, d), BF16)],
        compiler_params=pltpu.CompilerParams(vmem_limit_bytes=VMEM_LIMIT),
        name=f"post{layer}",
    )(o, x, mod, wo, ln_g, ln_b, mod, mod)


def _ffn_kernel(u_ref, x_ref, g_ref, w1_ref, w2_ref, lng_ref, lnb_ref, o_ref):
    u = u_ref[0]
    y = jnp.zeros(x_ref.shape[1:], F32)
    for c in range(D_FF // FF_CHUNK):
        sl = slice(c * FF_CHUNK, (c + 1) * FF_CHUNK)
        h = jnp.maximum(_dot(u, w1_ref[:, sl]), 0.0)
        y = y + _dot((h * h).astype(BF16), w2_ref[sl, :])
    z = DEEPNORM_ALPHA * x_ref[0] + (1.0 + g_ref[...]) * y
    o_ref[0] = _layer_norm_rows(z, lng_ref[...], lnb_ref[...])


def _ffn(u, x, mod, layer, w1, w2, ln_g, ln_b):
    b, s, d = x.shape
    return pl.pallas_call(
        _ffn_kernel,
        grid=(b, s // ROW_TILE),
        in_specs=[_row_spec(d), _row_spec(d), _mod_spec(layer, 5),
                  _whole_spec((d, D_FF)), _whole_spec((D_FF, d)),
                  _whole_spec((1, d)), _whole_spec((1, d))],
        out_specs=_row_spec(d),
        out_shape=jax.ShapeDtypeStruct((b, s, d), F32),
        compiler_params=pltpu.CompilerParams(vmem_limit_bytes=VMEM_LIMIT),
        name=f"ffn{layer}",
    )(u, x, mod, w1, w2, ln_g, ln_b)


def _rope_tables(pos_offset, s):
    half = MLA_ROPE // 2
    pos = pos_offset[:, None] + jnp.arange(s, dtype=jnp.int32)[None, :]
    inv = ROPE_THETA ** (-jnp.arange(half, dtype=F32) / half)
    ang = pos.astype(F32)[..., None] * inv
    cos = jnp.cos(ang)
    sin = jnp.sin(ang)
    b = pos.shape[0]
    ones = jnp.ones((b, s, MLA_NOPE), F32)
    zeros_tail = jnp.zeros((b, s, HEAD_LANES - MLA_NOPE - MLA_ROPE), F32)
    cos_t = jnp.concatenate([ones, cos, cos, zeros_tail], axis=-1)
    sin_t = jnp.concatenate([jnp.zeros_like(ones), -sin, sin, zeros_tail], axis=-1)
    return cos_t, sin_t


def _pad_heads(w, per_head, lead):
    w = w.reshape(lead, MLA_HEADS, per_head)
    w = jnp.pad(w, ((0, 0), (0, 0), (0, HEAD_LANES - per_head)))
    return w.reshape(lead, MLA_HEADS * HEAD_LANES)


def kernel(x, c, pos_offset, ada_w, ada_b, ln_g, ln_b, rel_table, da_w_qkv, da_w_o, da_lam_q1, da_lam_k1, da_lam_q2, da_lam_k2, da_subln_w, mla_w_down, mla_q_norm_w, mla_w_uq, mla_kv_norm_w, mla_w_ukv, mla_w_o, ffn_w1, ffn_w2):
    b, s, d = x.shape
    assert d == D_MODEL and s % ROW_TILE == 0 and ROW_TILE % ATTN_TILE == 0 and ATTN_TILE % CHUNK == 0

    mod = _modulation(c, ada_w, ada_b)

    w_qkv = da_w_qkv[0].astype(BF16)
    q, k, vt = _pre0(x, mod, w_qkv[:, :d], w_qkv[:, d:2 * d], w_qkv[:, 2 * d:].T)
    lam_vecs = jnp.stack([da_lam_q1[0], da_lam_k1[0], da_lam_q2[0], da_lam_k2[0]])
    lam_init = 0.8 - 0.6 * math.exp(-0.3 * 0)
    o = _attention(q, k, vt, _near_bias(rel_table), lam_vecs, da_subln_w[0].reshape(-1, 1),
                   diff=True, lam_init=lam_init)
    x, u = _post(o, x, mod, 0, da_w_o[0].astype(BF16), ln_g[0, 0][None], ln_b[0, 0][None])
    x = _ffn(u, x, mod, 0, ffn_w1[0].astype(BF16), ffn_w2[0].astype(BF16), ln_g[0, 1][None], ln_b[0, 1][None])

    n_lat = MLA_Q_LORA + MLA_KV_LORA
    w_down = mla_w_down[0]
    w_down = jnp.concatenate(
        [w_down[:, :n_lat], jnp.zeros((d, MLA_NOPE), F32), w_down[:, n_lat:],
         jnp.zeros((d, HEAD_LANES - MLA_NOPE - MLA_ROPE), F32)], axis=-1).astype(BF16)
    w_uq = _pad_heads(mla_w_uq[0], MLA_NOPE + MLA_ROPE, MLA_Q_LORA).astype(BF16)
    w_ukv = mla_w_ukv[0].reshape(MLA_KV_LORA, MLA_HEADS, MLA_NOPE + MLA_V)
    w_uk = _pad_heads(w_ukv[..., :MLA_NOPE].reshape(MLA_KV_LORA, -1), MLA_NOPE, MLA_KV_LORA).astype(BF16)
    w_uvt = w_ukv[..., MLA_NOPE:].reshape(MLA_KV_LORA, -1).T.astype(BF16)
    cos_t, sin_t = _rope_tables(pos_offset, s)
    q, k, vt = _pre1(x, mod, w_down, mla_q_norm_w[0][None], mla_kv_norm_w[0][None],
                     w_uq, w_uk, w_uvt, cos_t, sin_t)
    o = _attention(q, k, vt, _near_mask(), diff=False)
    x, u = _post(o, x, mod, 1, mla_w_o[0].astype(BF16), ln_g[1, 0][None], ln_b[1, 0][None])
    x = _ffn(u, x, mod, 1, ffn_w1[1].astype(BF16), ffn_w2[1].astype(BF16), ln_g[1, 1][None], ln_b[1, 1][None])
    return x
```

```python
import functools
import math

import jax
import jax.numpy as jnp
import numpy as np
from jax import lax
from jax.experimental import pallas as pl
from jax.experimental.pallas import tpu as pltpu

F32 = jnp.float32
BF16 = jnp.bfloat16

D_MODEL = 1024
DEPTH = 2
CHUNK = 64
DA_HEAD_DIM = 64
DA_HEADS = D_MODEL // (2 * DA_HEAD_DIM)
MLA_HEADS = 16
MLA_NOPE = 64
MLA_ROPE = 32
MLA_V = 64
MLA_Q_LORA = 512
MLA_KV_LORA = 256
ROPE_THETA = 10000.0
REL_BUCKETS = 32
REL_MAX_DIST = 128
D_FF = 4 * D_MODEL
DEEPNORM_ALPHA = (2 * DEPTH) ** 0.25
LN_EPS = 1e-5
RMS_EPS = 1e-6

LANES = 128
HEAD_LANES = 128
ATTN_TILE = 256
ATTN_UNROLL = 4
DOT_LEAD = 2
RETIRE_LAG = 3
ROW_TILE = 512
FF_CHUNK = 1024
LOG2E = math.log2(math.e)
NEG_BIG = -1e30
VMEM_LIMIT = 56 * 1024 * 1024

_NT = (((1,), (1,)), ((), ()))


def _dot(a, b):
    return jnp.dot(a, b, preferred_element_type=F32)


def _dot_nt(a, b):
    return lax.dot_general(a, b, _NT, preferred_element_type=F32)


def _layer_norm_rows(z, g, b):
    mu = jnp.mean(z, axis=-1, keepdims=True)
    zc = z - mu
    var = jnp.mean(zc * zc, axis=-1, keepdims=True)
    return zc * lax.rsqrt(var + LN_EPS) * g + b


def _rms_norm_rows(z, w):
    return z * lax.rsqrt(jnp.mean(z * z, axis=-1, keepdims=True) + RMS_EPS) * w


def _mod_kernel(c_ref, w_ref, b_ref, o_ref):
    c = c_ref[...]
    c_act = c * jax.nn.sigmoid(c)
    o_ref[0, 0] = _dot(c_act.astype(BF16), w_ref[0].astype(BF16)) + b_ref[0, 0]


def _modulation(c, ada_w, ada_b):
    b = c.shape[0]
    rows = 8
    c_pad = jnp.zeros((rows, D_MODEL), F32).at[:b].set(c)
    out = pl.pallas_call(
        _mod_kernel,
        grid=(DEPTH, 6),
        in_specs=[
            pl.BlockSpec((rows, D_MODEL), lambda i, j: (0, 0)),
            pl.BlockSpec((1, D_MODEL, D_MODEL), lambda i, j: (i, 0, j)),
            pl.BlockSpec((1, 1, 1, D_MODEL), lambda i, j: (i, j, 0, 0)),
        ],
        out_specs=pl.BlockSpec((1, 1, rows, D_MODEL), lambda i, j: (i, j, 0, 0)),
        out_shape=jax.ShapeDtypeStruct((DEPTH, 6, rows, D_MODEL), F32),
        compiler_params=pltpu.CompilerParams(vmem_limit_bytes=VMEM_LIMIT),
        name="mod",
    )(c_pad, ada_w, ada_b.reshape(DEPTH, 6, 1, D_MODEL))
    return out[:, :, :b].reshape(DEPTH, 6, b, 1, D_MODEL)


def _mod_spec(layer, idx):
    return pl.BlockSpec((None, None, None, 1, D_MODEL), lambda b, i: (layer, idx, b, 0, 0))


def _row_spec(width):
    return pl.BlockSpec((1, ROW_TILE, width), lambda b, i: (b, i, 0))


def _whole_spec(shape):
    return pl.BlockSpec(shape, lambda b, i: (0,) * len(shape))


def _vt_out_spec():
    return pl.BlockSpec((1, DA_HEADS, ROW_TILE // ATTN_TILE, HEAD_LANES, ATTN_TILE),
                        lambda b, i: (b, 0, i, 0, 0))


def _store_vt(vt_ref, vt):
    vt = vt.astype(BF16).reshape(DA_HEADS, HEAD_LANES, ROW_TILE)
    for t in range(ROW_TILE // ATTN_TILE):
        vt_ref[0, :, t] = vt[:, :, t * ATTN_TILE:(t + 1) * ATTN_TILE]


def _pre0_kernel(x_ref, sc_ref, sh_ref, wq_ref, wk_ref, wvt_ref, q_ref, k_ref, vt_ref, *, q_scale):
    u = (x_ref[0] * (1.0 + sc_ref[...]) + sh_ref[...]).astype(BF16)
    q_ref[0] = (_dot(u, wq_ref[...]) * q_scale).astype(BF16)
    k_ref[0] = _dot(u, wk_ref[...]).astype(BF16)
    _store_vt(vt_ref, _dot_nt(wvt_ref[...], u))


def _pre0(x, mod, wq, wk, wvt):
    b, s, d = x.shape
    q_scale = DA_HEAD_DIM ** -0.5 * LOG2E
    return pl.pallas_call(
        functools.partial(_pre0_kernel, q_scale=q_scale),
        grid=(b, s // ROW_TILE),
        in_specs=[_row_spec(d), _mod_spec(0, 1), _mod_spec(0, 0),
                  _whole_spec((d, d)), _whole_spec((d, d)), _whole_spec((d, d))],
        out_specs=[_row_spec(d), _row_spec(d), _vt_out_spec()],
        out_shape=[jax.ShapeDtypeStruct((b, s, d), BF16),
                   jax.ShapeDtypeStruct((b, s, d), BF16),
                   jax.ShapeDtypeStruct((b, DA_HEADS, s // ATTN_TILE, HEAD_LANES, ATTN_TILE), BF16)],
        compiler_params=pltpu.CompilerParams(vmem_limit_bytes=VMEM_LIMIT),
        name="pre0",
    )(x, mod, mod, wq, wk, wvt)


def _pre1_kernel(x_ref, sc_ref, sh_ref, wd_ref, qn_ref, kvn_ref, wuq_ref, wuk_ref, wuvt_ref,
                 cos_ref, sin_ref, q_ref, k_ref, vt_ref, *, q_scale):
    u = (x_ref[0] * (1.0 + sc_ref[...]) + sh_ref[...]).astype(BF16)
    down = _dot(u, wd_ref[...])
    c_q = _rms_norm_rows(down[:, :MLA_Q_LORA], qn_ref[...]).astype(BF16)
    c_kv = _rms_norm_rows(down[:, MLA_Q_LORA:MLA_Q_LORA + MLA_KV_LORA], kvn_ref[...]).astype(BF16)
    k_rope = down[:, MLA_Q_LORA + MLA_KV_LORA:]
    q = _dot(c_q, wuq_ref[...])
    k_nope = _dot(c_kv, wuk_ref[...])
    _store_vt(vt_ref, _dot_nt(wuvt_ref[...], c_kv))

    cos = cos_ref[0]
    sin = sin_ref[0]
    lane = lax.broadcasted_iota(jnp.int32, cos.shape, 1)
    first_half = lane < MLA_NOPE + MLA_ROPE // 2

    def rope(xh):
        partner = jnp.where(first_half,
                            pltpu.roll(xh, HEAD_LANES - MLA_ROPE // 2, 1),
                            pltpu.roll(xh, MLA_ROPE // 2, 1))
        return xh * cos + partner * sin

    k_rope = rope(k_rope)
    for h in range(MLA_HEADS):
        sl = slice(h * HEAD_LANES, (h + 1) * HEAD_LANES)
        q_ref[0, :, sl] = (rope(q[:, sl]) * q_scale).astype(BF16)
        k_ref[0, :, sl] = (k_nope[:, sl] + k_rope).astype(BF16)


def _pre1(x, mod, wd, qn, kvn, wuq, wuk, wuvt, cos_t, sin_t):
    b, s, d = x.shape
    hw = MLA_HEADS * HEAD_LANES
    q_scale = (MLA_NOPE + MLA_ROPE) ** -0.5 * LOG2E
    return pl.pallas_call(
        functools.partial(_pre1_kernel, q_scale=q_scale),
        grid=(b, s // ROW_TILE),
        in_specs=[_row_spec(d), _mod_spec(1, 1), _mod_spec(1, 0),
                  _whole_spec(wd.shape), _whole_spec(qn.shape), _whole_spec(kvn.shape),
                  _whole_spec(wuq.shape), _whole_spec(wuk.shape), _whole_spec(wuvt.shape),
                  _row_spec(HEAD_LANES), _row_spec(HEAD_LANES)],
        out_specs=[_row_spec(hw), _row_spec(hw), _vt_out_spec()],
        out_shape=[jax.ShapeDtypeStruct((b, s, hw), BF16),
                   jax.ShapeDtypeStruct((b, s, hw), BF16),
                   jax.ShapeDtypeStruct((b, DA_HEADS, s // ATTN_TILE, HEAD_LANES, ATTN_TILE), BF16)],
        compiler_params=pltpu.CompilerParams(vmem_limit_bytes=VMEM_LIMIT),
        name="pre1",
    )(x, mod, mod, wd, qn, kvn, wuq, wuk, wuvt, cos_t, sin_t)


def _t5_bucket(rel):
    nb = REL_BUCKETS // 2
    ret = (rel > 0).astype(jnp.int32) * nb
    n = jnp.abs(rel)
    max_exact = nb // 2
    is_small = n < max_exact
    n_f = jnp.maximum(n, 1).astype(F32)
    large = max_exact + (jnp.log(n_f / max_exact) / math.log(REL_MAX_DIST / max_exact)
                         * (nb - max_exact)).astype(jnp.int32)
    large = jnp.minimum(large, nb - 1)
    return ret + jnp.where(is_small, n, large)


def _near_bucket_tiles():
    kk = jnp.arange(ATTN_TILE, dtype=jnp.int32)[:, None]
    qq = jnp.arange(ATTN_TILE, dtype=jnp.int32)[None, :]
    diag = jnp.where(kk // CHUNK <= qq // CHUNK, _t5_bucket(kk - qq), -1)
    prev = _t5_bucket(kk - ATTN_TILE - qq)
    return jnp.stack([diag, prev])


def _bias_kernel(tab_ref, idx_ref, o_ref):
    h = pl.program_id(0)
    idx = idx_ref[...]
    val = jnp.zeros(idx.shape, F32)
    for bkt in range(REL_BUCKETS):
        val = jnp.where(idx == bkt, tab_ref[bkt, h], val)
    far = tab_ref[REL_BUCKETS // 2 - 1, h]
    o_ref[0] = jnp.where(idx < 0, NEG_BIG, (val - far) * LOG2E)


def _near_bias(rel_table):
    idx = _near_bucket_tiles()
    return pl.pallas_call(
        _bias_kernel,
        grid=(DA_HEADS,),
        in_specs=[pl.BlockSpec(memory_space=pltpu.SMEM),
                  pl.BlockSpec(idx.shape, lambda h: (0, 0, 0))],
        out_specs=pl.BlockSpec((1,) + idx.shape, lambda h: (h, 0, 0, 0)),
        out_shape=jax.ShapeDtypeStruct((DA_HEADS,) + idx.shape, F32),
        name="near_bias",
    )(rel_table, idx)


def _near_mask():
    kk = jnp.arange(ATTN_TILE, dtype=jnp.int32)[:, None]
    qq = jnp.arange(ATTN_TILE, dtype=jnp.int32)[None, :]
    return jnp.where(kk // CHUNK <= qq // CHUNK, 0.0, NEG_BIG).astype(F32)[None, None]


def _tile_tables(nq, n_near_kinds):
    ti, tj, ta = [], [], []
    for kind in range(n_near_kinds):
        for i in range(kind, nq):
            ti.append(i), tj.append(i - kind), ta.append(kind)
    n_near = len(ti)
    for j in range(nq - n_near_kinds):
        for i in range(j + n_near_kinds, nq):
            ti.append(i), tj.append(j), ta.append(0)
    return np.asarray(ti, np.int32), np.asarray(tj, np.int32), np.asarray(ta, np.int32), n_near


def _attn_kernel(*refs, diff, lam_init, n_near, n_tiles):
    ti_ref, tj_ref, ta_ref, q_ref, k_ref, vt_ref, near_ref = refs[:7]
    refs = refs[7:]
    if diff:
        lam_ref, subw_ref = refs[:2]
        refs = refs[2:]
    o_ref, s_buf, p_buf, mx_buf, al_buf, m_scr, l_scr, acc = refs[:8]
    t = ATTN_TILE
    unroll = s_buf.shape[0]
    nq = acc.shape[1]
    dv = acc.shape[2]

    if diff:
        qm = refs[8]
        q_all = q_ref[0]
        lane = lax.broadcasted_iota(jnp.int32, q_all.shape, 1)
        zero = jnp.zeros_like(q_all)
        qm[0] = jnp.where(lane < DA_HEAD_DIM, q_all, zero)
        qm[1] = jnp.where(lane >= DA_HEAD_DIM, q_all, zero)

    m_scr[...] = jnp.full(m_scr.shape, NEG_BIG, F32)
    l_scr[...] = jnp.zeros(l_scr.shape, F32)
    acc[...] = jnp.zeros(acc.shape, F32)

    def rows(idx):
        return pl.ds(pl.multiple_of(idx * t, t), t)

    def q_map(mp, i):
        if diff:
            return qm[mp, rows(i), :]
        return q_ref[0, rows(i), mp * HEAD_LANES:(mp + 1) * HEAD_LANES]

    def score_dot(n, mp):
        i, j = ti_ref[n], tj_ref[n]
        k_m = k_ref[0, rows(j), :] if diff else k_ref[0, rows(j), mp * HEAD_LANES:(mp + 1) * HEAD_LANES]
        return _dot_nt(k_m, q_map(mp, i))

    def score_store(n, u, mp, s, near):
        if near:
            s = s + near_ref[0, ta_ref[n]]
        s_buf[u, mp] = s
        mx_buf[u, mp] = jnp.max(s, axis=0, keepdims=True)

    def exps(n, u, mp):
        i = ti_ref[n]
        m_old = m_scr[mp, i]
        m_new = jnp.maximum(m_old, mx_buf[u, mp])
        alpha = jnp.exp2(m_old - m_new)
        p = jnp.exp2(s_buf[u, mp] - m_new)
        l_scr[mp, i] = alpha * l_scr[mp, i] + jnp.sum(p, axis=0, keepdims=True)
        m_scr[mp, i] = m_new
        p_buf[u, mp] = p.astype(BF16)
        al_buf[u, mp] = alpha

    def pv_dot(n, u, mp):
        j = tj_ref[n]
        vt_m = vt_ref[0, 0, j] if diff else vt_ref[0, 0, j, mp * dv:(mp + 1) * dv, :]
        return _dot(vt_m, p_buf[u, mp]), al_buf[u, mp]

    def acc_update(n, mp, pv, alpha):
        i = ti_ref[n]
        acc[mp, i] = alpha * acc[mp, i] + pv

    items = [(u, mp) for u in range(unroll) for mp in range(2)]

    def step(trip, near_flags, has1=True, has2=True, has3=True):
        pending = {}

        def issue(kk):
            u, mp = items[kk]
            pv = pv_dot(unroll * (trip - 2) + u, u, mp) if has3 else None
            s = score_dot(unroll * trip + u, mp) if has1 else None
            pending[kk] = (s, pv)

        def retire(kk):
            u, mp = items[kk]
            s, pv = pending.pop(kk)
            if has1:
                score_store(unroll * trip + u, u, mp, s, near_flags[u])
            if has3:
                acc_update(unroll * (trip - 2) + u, mp, *pv)

        for kk in range(min(DOT_LEAD, len(items))):
            issue(kk)
        for kk, (u, mp) in enumerate(items):
            if has2:
                exps(unroll * (trip - 1) + u, u, mp)
            if kk + DOT_LEAD < len(items):
                issue(kk + DOT_LEAD)
            if kk + DOT_LEAD - RETIRE_LAG >= 0:
                retire(kk + DOT_LEAD - RETIRE_LAG)
        for kk in sorted(pending):
            retire(kk)

    n_trips = n_tiles // unroll

    def near_flags(trip):
        return [unroll * trip + u < n_near for u in range(unroll)]

    def uniform(trip):
        flags = near_flags(trip)
        return all(flags) or not any(flags)

    step(0, near_flags(0), has2=False, has3=False)
    step(1, near_flags(1), has3=False)
    trip = 2
    while trip < n_trips:
        end = trip + 1
        while uniform(trip) and end < n_trips and near_flags(end) == near_flags(trip):
            end += 1
        if end - trip >= 2:
            flags = near_flags(trip)
            lax.fori_loop(trip, end, lambda tt, carry: (step(tt, flags), carry)[1], 0)
        else:
            step(trip, near_flags(trip))
        trip = end
    step(n_trips, None, has1=False)
    step(n_trips + 1, None, has1=False, has2=False)

    if diff:
        lam = (jnp.exp(jnp.sum(lam_ref[0:1] * lam_ref[1:2], axis=-1, keepdims=True))
               - jnp.exp(jnp.sum(lam_ref[2:3] * lam_ref[3:4], axis=-1, keepdims=True)) + lam_init)
        sub_w = subw_ref[...] * (1.0 - lam_init)

    def finalize(i, carry):
        o_a = acc[0, i] / l_scr[0, i]
        o_b = acc[1, i] / l_scr[1, i]
        if diff:
            o = o_a - lam * o_b
            o = o * lax.rsqrt(jnp.mean(o * o, axis=0, keepdims=True) + RMS_EPS) * sub_w
        else:
            o = jnp.concatenate([o_a, o_b], axis=0)
        o_ref[0, rows(i), :] = o.T.astype(BF16)
        return carry

    lax.fori_loop(0, nq, finalize, 0)


def _attention(q, k, vt, near, lam_vecs=None, subln_w=None, *, diff, lam_init=0.0):
    b, s, _ = q.shape
    t = ATTN_TILE
    nq = s // t
    heads = DA_HEADS
    width = HEAD_LANES if diff else 2 * HEAD_LANES
    dv = HEAD_LANES if diff else MLA_V
    near_heads = near.shape[0]
    ti, tj, ta, n_near = _tile_tables(nq, near.shape[1])
    n_tiles = len(ti)
    assert n_tiles % ATTN_UNROLL == 0 and n_tiles // ATTN_UNROLL >= 2
    smem = pl.BlockSpec(memory_space=pltpu.SMEM)
    in_specs = [
        smem, smem, smem,
        pl.BlockSpec((1, s, width), lambda bi, h: (bi, 0, h)),
        pl.BlockSpec((1, s, width), lambda bi, h: (bi, 0, h)),
        pl.BlockSpec((1, 1, nq, HEAD_LANES, t), lambda bi, h: (bi, h, 0, 0, 0)),
        pl.BlockSpec((1,) + near.shape[1:], lambda bi, h: (h if near_heads > 1 else 0, 0, 0, 0)),
    ]
    args = [jnp.asarray(ti), jnp.asarray(tj), jnp.asarray(ta), q, k, vt, near]
    if diff:
        in_specs += [pl.BlockSpec(lam_vecs.shape, lambda bi, h: (0, 0)),
                     pl.BlockSpec(subln_w.shape, lambda bi, h: (0, 0))]
        args += [lam_vecs, subln_w]
    un = ATTN_UNROLL
    scratch = [pltpu.VMEM((un, 2, t, t), F32), pltpu.VMEM((un, 2, t, t), BF16),
               pltpu.VMEM((un, 2, 1, t), F32), pltpu.VMEM((un, 2, 1, t), F32),
               pltpu.VMEM((2, nq, 1, t), F32), pltpu.VMEM((2, nq, 1, t), F32),
               pltpu.VMEM((2, nq, dv, t), F32)]
    if diff:
        scratch.append(pltpu.VMEM((2, s, HEAD_LANES), BF16))
    return pl.pallas_call(
        functools.partial(_attn_kernel, diff=diff, lam_init=lam_init, n_near=n_near, n_tiles=n_tiles),
        grid=(b, heads),
        in_specs=in_specs,
        out_specs=pl.BlockSpec((1, s, HEAD_LANES), lambda bi, h: (bi, 0, h)),
        out_shape=jax.ShapeDtypeStruct((b, s, heads * HEAD_LANES), BF16),
        scratch_shapes=scratch,
        compiler_params=pltpu.CompilerParams(vmem_limit_bytes=VMEM_LIMIT),
        name="attn_diff" if diff else "attn_mla",
    )(*args)


def _post_kernel(o_ref, x_ref, g_ref, wo_ref, lng_ref, lnb_ref, sc_ref, sh_ref, x1_ref, u_ref):
    y = _dot(o_ref[0], wo_ref[...])
    z = DEEPNORM_ALPHA * x_ref[0] + (1.0 + g_ref[...]) * y
    xn = _layer_norm_rows(z, lng_ref[...], lnb_ref[...])
    x1_ref[0] = xn
    u_ref[0] = (xn * (1.0 + sc_ref[...]) + sh_ref[...]).astype(BF16)


def _post(o, x, mod, layer, wo, ln_g, ln_b):
    b, s, d = x.shape
    return pl.pallas_call(
        _post_kernel,
        grid=(b, s // ROW_TILE),
        in_specs=[_row_spec(d), _row_spec(d), _mod_spec(layer, 2), _whole_spec((d, d)),
                  _whole_spec((1, d)), _whole_spec((1, d)), _mod_spec(layer, 4), _mod_spec(layer, 3)],
        out_specs=[_row_spec(d), _row_spec(d)],
        out_shape=[jax.ShapeDtypeStruct((b, s, d), F32), jax.ShapeDtypeStruct((b, s, d), BF16)],
        compiler_params=pltpu.CompilerParams(vmem_limit_bytes=VMEM_LIMIT),
        name=f"post{layer}",
    )(o, x, mod, wo, ln_g, ln_b, mod, mod)


def _ffn_kernel(u_ref, x_ref, g_ref, w1_ref, w2_ref, lng_ref, lnb_ref, o_ref):
    u = u_ref[0]
    y = jnp.zeros(x_ref.shape[1:], F32)
    for c in range(D_FF // FF_CHUNK):
        sl = slice(c * FF_CHUNK, (c + 1) * FF_CHUNK)
        h = jnp.maximum(_dot(u, w1_ref[:, sl]), 0.0)
        y = y + _dot((h * h).astype(BF16), w2_ref[sl, :])
    z = DEEPNORM_ALPHA * x_ref[0] + (1.0 + g_ref[...]) * y
    o_ref[0] = _layer_norm_rows(z, lng_ref[...], lnb_ref[...])


def _ffn(u, x, mod, layer, w1, w2, ln_g, ln_b):
    b, s, d = x.shape
    return pl.pallas_call(
        _ffn_kernel,
        grid=(b, s // ROW_TILE),
        in_specs=[_row_spec(d), _row_spec(d), _mod_spec(layer, 5),
                  _whole_spec((d, D_FF)), _whole_spec((D_FF, d)),
                  _whole_spec((1, d)), _whole_spec((1, d))],
        out_specs=_row_spec(d),
        out_shape=jax.ShapeDtypeStruct((b, s, d), F32),
        compiler_params=pltpu.CompilerParams(vmem_limit_bytes=VMEM_LIMIT),
        name=f"ffn{layer}",
    )(u, x, mod, w1, w2, ln_g, ln_b)


def _rope_tables(pos_offset, s):
    half = MLA_ROPE // 2
    pos = pos_offset[:, None] + jnp.arange(s, dtype=jnp.int32)[None, :]
    inv = ROPE_THETA ** (-jnp.arange(half, dtype=F32) / half)
    ang = pos.astype(F32)[..., None] * inv
    cos = jnp.cos(ang)
    sin = jnp.sin(ang)
    b = pos.shape[0]
    ones = jnp.ones((b, s, MLA_NOPE), F32)
    zeros_tail = jnp.zeros((b, s, HEAD_LANES - MLA_NOPE - MLA_ROPE), F32)
    cos_t = jnp.concatenate([ones, cos, cos, zeros_tail], axis=-1)
    sin_t = jnp.concatenate([jnp.zeros_like(ones), -sin, sin, zeros_tail], axis=-1)
    return cos_t, sin_t


def _pad_heads(w, per_head, lead):
    w = w.reshape(lead, MLA_HEADS, per_head)
    w = jnp.pad(w, ((0, 0), (0, 0), (0, HEAD_LANES - per_head)))
    return w.reshape(lead, MLA_HEADS * HEAD_LANES)


def kernel(x, c, pos_offset, ada_w, ada_b, ln_g, ln_b, rel_table, da_w_qkv, da_w_o, da_lam_q1, da_lam_k1, da_lam_q2, da_lam_k2, da_subln_w, mla_w_down, mla_q_norm_w, mla_w_uq, mla_kv_norm_w, mla_w_ukv, mla_w_o, ffn_w1, ffn_w2):
    b, s, d = x.shape
    assert d == D_MODEL and s % ROW_TILE == 0 and ROW_TILE % ATTN_TILE == 0 and ATTN_TILE % CHUNK == 0

    mod = _modulation(c, ada_w, ada_b)

    w_qkv = da_w_qkv[0].astype(BF16)
    q, k, vt = _pre0(x, mod, w_qkv[:, :d], w_qkv[:, d:2 * d], w_qkv[:, 2 * d:].T)
    lam_vecs = jnp.stack([da_lam_q1[0], da_lam_k1[0], da_lam_q2[0], da_lam_k2[0]])
    lam_init = 0.8 - 0.6 * math.exp(-0.3 * 0)
    o = _attention(q, k, vt, _near_bias(rel_table), lam_vecs, da_subln_w[0].reshape(-1, 1),
                   diff=True, lam_init=lam_init)
    x, u = _post(o, x, mod, 0, da_w_o[0].astype(BF16), ln_g[0, 0][None], ln_b[0, 0][None])
    x = _ffn(u, x, mod, 0, ffn_w1[0].astype(BF16), ffn_w2[0].astype(BF16), ln_g[0, 1][None], ln_b[0, 1][None])

    n_lat = MLA_Q_LORA + MLA_KV_LORA
    w_down = mla_w_down[0]
    w_down = jnp.concatenate(
        [w_down[:, :n_lat], jnp.zeros((d, MLA_NOPE), F32), w_down[:, n_lat:],
         jnp.zeros((d, HEAD_LANES - MLA_NOPE - MLA_ROPE), F32)], axis=-1).astype(BF16)
    w_uq = _pad_heads(mla_w_uq[0], MLA_NOPE + MLA_ROPE, MLA_Q_LORA).astype(BF16)
    w_ukv = mla_w_ukv[0].reshape(MLA_KV_LORA, MLA_HEADS, MLA_NOPE + MLA_V)
    w_uk = _pad_heads(w_ukv[..., :MLA_NOPE].reshape(MLA_KV_LORA, -1), MLA_NOPE, MLA_KV_LORA).astype(BF16)
    w_uvt = w_ukv[..., MLA_NOPE:].reshape(MLA_KV_LORA, -1).T.astype(BF16)
    cos_t, sin_t = _rope_tables(pos_offset, s)
    q, k, vt = _pre1(x, mod, w_down, mla_q_norm_w[0][None], mla_kv_norm_w[0][None],
                     w_uq, w_uk, w_uvt, cos_t, sin_t)
    o = _attention(q, k, vt, _near_mask(), diff=False)
    x, u = _post(o, x, mod, 1, mla_w_o[0].astype(BF16), ln_g[1, 0][None], ln_b[1, 0][None])
    x = _ffn(u, x, mod, 1, ffn_w1[1].astype(BF16), ffn_w2[1].astype(BF16), ln_g[1, 1][None], ln_b[1, 1][None])
    return x
```

```python
import functools
import math

import jax
import jax.numpy as jnp
import numpy as np
from jax import lax
from jax.experimental import pallas as pl
from jax.experimental.pallas import tpu as pltpu

F32 = jnp.float32
BF16 = jnp.bfloat16

D_MODEL = 1024
DEPTH = 2
CHUNK = 64
DA_HEAD_DIM = 64
DA_HEADS = D_MODEL // (2 * DA_HEAD_DIM)
MLA_HEADS = 16
MLA_NOPE = 64
MLA_ROPE = 32
MLA_V = 64
MLA_Q_LORA = 512
MLA_KV_LORA = 256
ROPE_THETA = 10000.0
REL_BUCKETS = 32
REL_MAX_DIST = 128
D_FF = 4 * D_MODEL
DEEPNORM_ALPHA = (2 * DEPTH) ** 0.25
LN_EPS = 1e-5
RMS_EPS = 1e-6

LANES = 128
HEAD_LANES = 128
ATTN_TILE = 256
ATTN_UNROLL = 8
DOT_LEAD = 2
RETIRE_LAG = 3
SUM_ROWS = 16
ROW_TILE = 512
FF_CHUNK = 1024
LOG2E = math.log2(math.e)
NEG_BIG = -1e30
VMEM_LIMIT = 56 * 1024 * 1024

_NT = (((1,), (1,)), ((), ()))


def _dot(a, b):
    return jnp.dot(a, b, preferred_element_type=F32)


def _dot_nt(a, b):
    return lax.dot_general(a, b, _NT, preferred_element_type=F32)


def _layer_norm_rows(z, g, b):
    mu = jnp.mean(z, axis=-1, keepdims=True)
    zc = z - mu
    var = jnp.mean(zc * zc, axis=-1, keepdims=True)
    return zc * lax.rsqrt(var + LN_EPS) * g + b


def _rms_norm_rows(z, w):
    return z * lax.rsqrt(jnp.mean(z * z, axis=-1, keepdims=True) + RMS_EPS) * w


def _mod_kernel(c_ref, w_ref, b_ref, o_ref):
    c = c_ref[...]
    c_act = c * jax.nn.sigmoid(c)
    o_ref[0, 0] = _dot(c_act.astype(BF16), w_ref[0].astype(BF16)) + b_ref[0, 0]


def _modulation(c, ada_w, ada_b):
    b = c.shape[0]
    rows = 8
    c_pad = jnp.zeros((rows, D_MODEL), F32).at[:b].set(c)
    out = pl.pallas_call(
        _mod_kernel,
        grid=(DEPTH, 6),
        in_specs=[
            pl.BlockSpec((rows, D_MODEL), lambda i, j: (0, 0)),
            pl.BlockSpec((1, D_MODEL, D_MODEL), lambda i, j: (i, 0, j)),
            pl.BlockSpec((1, 1, 1, D_MODEL), lambda i, j: (i, j, 0, 0)),
        ],
        out_specs=pl.BlockSpec((1, 1, rows, D_MODEL), lambda i, j: (i, j, 0, 0)),
        out_shape=jax.ShapeDtypeStruct((DEPTH, 6, rows, D_MODEL), F32),
        compiler_params=pltpu.CompilerParams(vmem_limit_bytes=VMEM_LIMIT),
        name="mod",
    )(c_pad, ada_w, ada_b.reshape(DEPTH, 6, 1, D_MODEL))
    return out[:, :, :b].reshape(DEPTH, 6, b, 1, D_MODEL)


def _mod_spec(layer, idx):
    return pl.BlockSpec((None, None, None, 1, D_MODEL), lambda b, i: (layer, idx, b, 0, 0))


def _row_spec(width):
    return pl.BlockSpec((1, ROW_TILE, width), lambda b, i: (b, i, 0))


def _whole_spec(shape):
    return pl.BlockSpec(shape, lambda b, i: (0,) * len(shape))


def _vt_rows(maps_per_head):
    return HEAD_LANES + maps_per_head * SUM_ROWS


def _vt_shape(b, s, maps_per_head):
    return (b, DA_HEADS, s // ATTN_TILE, _vt_rows(maps_per_head), ATTN_TILE)


def _vt_out_spec(maps_per_head):
    return pl.BlockSpec((1, DA_HEADS, ROW_TILE // ATTN_TILE, _vt_rows(maps_per_head), ATTN_TILE),
                        lambda b, i: (b, 0, i, 0, 0))


def _store_vt(vt_ref, vt, maps_per_head):
    groups = DA_HEADS * maps_per_head
    vt = vt.astype(BF16).reshape(groups, HEAD_LANES // maps_per_head, ROW_TILE)
    ones = jnp.ones((groups, SUM_ROWS, ROW_TILE), BF16)
    vt = jnp.concatenate([vt, ones], axis=1).reshape(DA_HEADS, _vt_rows(maps_per_head), ROW_TILE)
    for t in range(ROW_TILE // ATTN_TILE):
        vt_ref[0, :, t] = vt[:, :, t * ATTN_TILE:(t + 1) * ATTN_TILE]


def _pre0_kernel(x_ref, sc_ref, sh_ref, wq_ref, wk_ref, wvt_ref, q_ref, k_ref, vt_ref, *, q_scale):
    u = (x_ref[0] * (1.0 + sc_ref[...]) + sh_ref[...]).astype(BF16)
    q_ref[0] = (_dot(u, wq_ref[...]) * q_scale).astype(BF16)
    k_ref[0] = _dot(u, wk_ref[...]).astype(BF16)
    _store_vt(vt_ref, _dot_nt(wvt_ref[...], u), 1)


def _pre0(x, mod, wq, wk, wvt):
    b, s, d = x.shape
    q_scale = DA_HEAD_DIM ** -0.5 * LOG2E
    return pl.pallas_call(
        functools.partial(_pre0_kernel, q_scale=q_scale),
        grid=(b, s // ROW_TILE),
        in_specs=[_row_spec(d), _mod_spec(0, 1), _mod_spec(0, 0),
                  _whole_spec((d, d)), _whole_spec((d, d)), _whole_spec((d, d))],
        out_specs=[_row_spec(d), _row_spec(d), _vt_out_spec(1)],
        out_shape=[jax.ShapeDtypeStruct((b, s, d), BF16),
                   jax.ShapeDtypeStruct((b, s, d), BF16),
                   jax.ShapeDtypeStruct(_vt_shape(b, s, 1), BF16)],
        compiler_params=pltpu.CompilerParams(vmem_limit_bytes=VMEM_LIMIT),
        name="pre0",
    )(x, mod, mod, wq, wk, wvt)


def _pre1_kernel(x_ref, sc_ref, sh_ref, wd_ref, qn_ref, kvn_ref, wuq_ref, wuk_ref, wuvt_ref,
                 cos_ref, sin_ref, q_ref, k_ref, vt_ref, *, q_scale):
    u = (x_ref[0] * (1.0 + sc_ref[...]) + sh_ref[...]).astype(BF16)
    down = _dot(u, wd_ref[...])
    c_q = _rms_norm_rows(down[:, :MLA_Q_LORA], qn_ref[...]).astype(BF16)
    c_kv = _rms_norm_rows(down[:, MLA_Q_LORA:MLA_Q_LORA + MLA_KV_LORA], kvn_ref[...]).astype(BF16)
    k_rope = down[:, MLA_Q_LORA + MLA_KV_LORA:]
    q = _dot(c_q, wuq_ref[...])
    k_nope = _dot(c_kv, wuk_ref[...])
    _store_vt(vt_ref, _dot_nt(wuvt_ref[...], c_kv), 2)

    cos = cos_ref[0]
    sin = sin_ref[0]
    lane = lax.broadcasted_iota(jnp.int32, cos.shape, 1)
    first_half = lane < MLA_NOPE + MLA_ROPE // 2

    def rope(xh):
        partner = jnp.where(first_half,
                            pltpu.roll(xh, HEAD_LANES - MLA_ROPE // 2, 1),
                            pltpu.roll(xh, MLA_ROPE // 2, 1))
        return xh * cos + partner * sin

    k_rope = rope(k_rope)
    for h in range(MLA_HEADS):
        sl = slice(h * HEAD_LANES, (h + 1) * HEAD_LANES)
        q_ref[0, :, sl] = (rope(q[:, sl]) * q_scale).astype(BF16)
        k_ref[0, :, sl] = (k_nope[:, sl] + k_rope).astype(BF16)


def _pre1(x, mod, wd, qn, kvn, wuq, wuk, wuvt, cos_t, sin_t):
    b, s, d = x.shape
    hw = MLA_HEADS * HEAD_LANES
    q_scale = (MLA_NOPE + MLA_ROPE) ** -0.5 * LOG2E
    return pl.pallas_call(
        functools.partial(_pre1_kernel, q_scale=q_scale),
        grid=(b, s // ROW_TILE),
        in_specs=[_row_spec(d), _mod_spec(1, 1), _mod_spec(1, 0),
                  _whole_spec(wd.shape), _whole_spec(qn.shape), _whole_spec(kvn.shape),
                  _whole_spec(wuq.shape), _whole_spec(wuk.shape), _whole_spec(wuvt.shape),
                  _row_spec(HEAD_LANES), _row_spec(HEAD_LANES)],
        out_specs=[_row_spec(hw), _row_spec(hw), _vt_out_spec(2)],
        out_shape=[jax.ShapeDtypeStruct((b, s, hw), BF16),
                   jax.ShapeDtypeStruct((b, s, hw), BF16),
                   jax.ShapeDtypeStruct(_vt_shape(b, s, 2), BF16)],
        compiler_params=pltpu.CompilerParams(vmem_limit_bytes=VMEM_LIMIT),
        name="pre1",
    )(x, mod, mod, wd, qn, kvn, wuq, wuk, wuvt, cos_t, sin_t)


def _t5_bucket(rel):
    nb = REL_BUCKETS // 2
    max_exact = nb // 2
    n_log = nb - max_exact
    thresholds = []
    for k in range(1, n_log):
        n = max_exact
        while n ** n_log * max_exact ** k < REL_MAX_DIST ** k * max_exact ** n_log:
            n += 1
        thresholds.append(n)
    n = np.abs(rel)
    large = max_exact + sum((n >= thr).astype(np.int32) for thr in thresholds)
    return (rel > 0).astype(np.int32) * nb + np.where(n < max_exact, n, large)


def _near_bucket_tiles():
    kk = np.arange(ATTN_TILE, dtype=np.int32)[:, None]
    qq = np.arange(ATTN_TILE, dtype=np.int32)[None, :]
    diag = np.where(kk // CHUNK <= qq // CHUNK, _t5_bucket(kk - qq), -1)
    prev = _t5_bucket(kk - ATTN_TILE - qq)
    return jnp.asarray(np.stack([diag, prev]).astype(np.int32))


def _bias_kernel(tab_ref, idx_ref, o_ref):
    h = pl.program_id(0)
    idx = idx_ref[...]
    val = jnp.zeros(idx.shape, F32)
    for bkt in range(REL_BUCKETS):
        val = jnp.where(idx == bkt, tab_ref[bkt, h], val)
    far = tab_ref[REL_BUCKETS // 2 - 1, h]
    o_ref[0] = jnp.where(idx < 0, NEG_BIG, (val - far) * LOG2E)


def _near_bias(rel_table):
    idx = _near_bucket_tiles()
    return pl.pallas_call(
        _bias_kernel,
        grid=(DA_HEADS,),
        in_specs=[pl.BlockSpec(memory_space=pltpu.SMEM),
                  pl.BlockSpec(idx.shape, lambda h: (0, 0, 0))],
        out_specs=pl.BlockSpec((1,) + idx.shape, lambda h: (h, 0, 0, 0)),
        out_shape=jax.ShapeDtypeStruct((DA_HEADS,) + idx.shape, F32),
        name="near_bias",
    )(rel_table, idx)


def _near_mask():
    kk = jnp.arange(ATTN_TILE, dtype=jnp.int32)[:, None]
    qq = jnp.arange(ATTN_TILE, dtype=jnp.int32)[None, :]
    return jnp.where(kk // CHUNK <= qq // CHUNK, 0.0, NEG_BIG).astype(F32)[None, None]


def _tile_tables(nq, n_near_kinds):
    ti, tj, ta = [], [], []
    for kind in range(n_near_kinds):
        for i in range(kind, nq):
            ti.append(i), tj.append(i - kind), ta.append(kind)
    n_near = len(ti)
    for j in range(nq - n_near_kinds):
        for i in range(j + n_near_kinds, nq):
            ti.append(i), tj.append(j), ta.append(0)
    return np.asarray(ti, np.int32), np.asarray(tj, np.int32), np.asarray(ta, np.int32), n_near


def _attn_kernel(*refs, diff, lam_init, n_near, n_tiles):
    ti_ref, tj_ref, ta_ref, q_ref, k_ref, vt_ref, near_ref = refs[:7]
    refs = refs[7:]
    if diff:
        lam_ref, subw_ref = refs[:2]
        refs = refs[2:]
    o_ref, s_buf, p_buf, mx_buf, al_buf, m_scr, acc = refs[:7]
    t = ATTN_TILE
    unroll = s_buf.shape[0]
    nq = acc.shape[1]
    acc_rows = acc.shape[2]
    dv = acc_rows - SUM_ROWS

    if diff:
        qm = refs[7]
        q_all = q_ref[0]
        lane = lax.broadcasted_iota(jnp.int32, q_all.shape, 1)
        zero = jnp.zeros_like(q_all)
        qm[0] = jnp.where(lane < DA_HEAD_DIM, q_all, zero)
        qm[1] = jnp.where(lane >= DA_HEAD_DIM, q_all, zero)

    m_scr[...] = jnp.full(m_scr.shape, NEG_BIG, F32)
    acc[...] = jnp.zeros(acc.shape, F32)

    def rows(idx):
        return pl.ds(pl.multiple_of(idx * t, t), t)

    def q_map(mp, i):
        if diff:
            return qm[mp, rows(i), :]
        return q_ref[0, rows(i), mp * HEAD_LANES:(mp + 1) * HEAD_LANES]

    def score_dot(n, mp):
        i, j = ti_ref[n], tj_ref[n]
        k_m = k_ref[0, rows(j), :] if diff else k_ref[0, rows(j), mp * HEAD_LANES:(mp + 1) * HEAD_LANES]
        return _dot_nt(k_m, q_map(mp, i))

    def score_store(n, u, mp, s, near):
        s_buf[u, mp] = s
        if near:
            s = s_buf[u, mp] + near_ref[0, ta_ref[n]]
            s_buf[u, mp] = s
        mx_buf[u, mp] = jnp.max(s, axis=0, keepdims=True)

    def exps(n, u, mp):
        i = ti_ref[n]
        m_old = m_scr[mp, i]
        m_new = jnp.maximum(m_old, mx_buf[u, mp])
        alpha = jnp.exp2(m_old - m_new)
        p = jnp.exp2(s_buf[u, mp] - m_new)
        m_scr[mp, i] = m_new
        p_buf[u, mp] = p.astype(BF16)
        al_buf[u, mp] = alpha

    def pv_dot(n, u, mp):
        j = tj_ref[n]
        vt_m = vt_ref[0, 0, j] if diff else vt_ref[0, 0, j, mp * acc_rows:(mp + 1) * acc_rows, :]
        return _dot(vt_m, p_buf[u, mp]), al_buf[u, mp]

    def acc_update(n, mp, pv, alpha):
        i = ti_ref[n]
        acc[mp, i] = alpha * acc[mp, i] + pv

    items = [(u, mp) for u in range(unroll) for mp in range(2)]

    def step(trip, near_flags, has1=True, has2=True, has3=True):
        pending = {}

        def issue(kk):
            u, mp = items[kk]
            pv = pv_dot(unroll * (trip - 2) + u, u, mp) if has3 else None
            s = score_dot(unroll * trip + u, mp) if has1 else None
            pending[kk] = (s, pv)

        def retire(kk):
            u, mp = items[kk]
            s, pv = pending.pop(kk)
            if has1:
                score_store(unroll * trip + u, u, mp, s, near_flags[u])
            if has3:
                acc_update(unroll * (trip - 2) + u, mp, *pv)

        for kk in range(min(DOT_LEAD, len(items))):
            issue(kk)
        for kk, (u, mp) in enumerate(items):
            if has2:
                exps(unroll * (trip - 1) + u, u, mp)
            if kk + DOT_LEAD < len(items):
                issue(kk + DOT_LEAD)
            if kk + DOT_LEAD - RETIRE_LAG >= 0:
                retire(kk + DOT_LEAD - RETIRE_LAG)
        for kk in sorted(pending):
            retire(kk)

    n_trips = n_tiles // unroll

    def near_flags(trip):
        return [unroll * trip + u < n_near for u in range(unroll)]

    def uniform(trip):
        flags = near_flags(trip)
        return all(flags) or not any(flags)

    step(0, near_flags(0), has2=False, has3=False)
    step(1, near_flags(1), has3=False)
    trip = 2
    while trip < n_trips:
        end = trip + 1
        while uniform(trip) and end < n_trips and near_flags(end) == near_flags(trip):
            end += 1
        if end - trip >= 2:
            flags = near_flags(trip)
            lax.fori_loop(trip, end, lambda tt, carry: (step(tt, flags), carry)[1], 0)
        else:
            step(trip, near_flags(trip))
        trip = end
    step(n_trips, None, has1=False)
    step(n_trips + 1, None, has1=False, has2=False)

    if diff:
        lam = (jnp.exp(jnp.sum(lam_ref[0:1] * lam_ref[1:2], axis=-1, keepdims=True))
               - jnp.exp(jnp.sum(lam_ref[2:3] * lam_ref[3:4], axis=-1, keepdims=True)) + lam_init)
        sub_w = subw_ref[...] * (1.0 - lam_init)

    def finalize(i, carry):
        o_a = acc[0, i, :dv] / acc[0, i, dv:dv + 1]
        o_b = acc[1, i, :dv] / acc[1, i, dv:dv + 1]
        if diff:
            o = o_a - lam * o_b
            o = o * lax.rsqrt(jnp.mean(o * o, axis=0, keepdims=True) + RMS_EPS) * sub_w
        else:
            o = jnp.concatenate([o_a, o_b], axis=0)
        o_ref[0, rows(i), :] = o.T.astype(BF16)
        return carry

    lax.fori_loop(0, nq, finalize, 0)


def _attention(q, k, vt, near, lam_vecs=None, subln_w=None, *, diff, lam_init=0.0):
    b, s, _ = q.shape
    t = ATTN_TILE
    nq = s // t
    heads = DA_HEADS
    width = HEAD_LANES if diff else 2 * HEAD_LANES
    acc_rows = (HEAD_LANES if diff else MLA_V) + SUM_ROWS
    assert vt.shape[3] == (1 if diff else 2) * acc_rows
    near_heads = near.shape[0]
    ti, tj, ta, n_near = _tile_tables(nq, near.shape[1])
    n_tiles = len(ti)
    assert n_tiles % ATTN_UNROLL == 0 and n_tiles // ATTN_UNROLL >= 2
    smem = pl.BlockSpec(memory_space=pltpu.SMEM)
    in_specs = [
        smem, smem, smem,
        pl.BlockSpec((1, s, width), lambda bi, h: (bi, 0, h)),
        pl.BlockSpec((1, s, width), lambda bi, h: (bi, 0, h)),
        pl.BlockSpec((1, 1, nq, vt.shape[3], t), lambda bi, h: (bi, h, 0, 0, 0)),
        pl.BlockSpec((1,) + near.shape[1:], lambda bi, h: (h if near_heads > 1 else 0, 0, 0, 0)),
    ]
    args = [jnp.asarray(ti), jnp.asarray(tj), jnp.asarray(ta), q, k, vt, near]
    if diff:
        in_specs += [pl.BlockSpec(lam_vecs.shape, lambda bi, h: (0, 0)),
                     pl.BlockSpec(subln_w.shape, lambda bi, h: (0, 0))]
        args += [lam_vecs, subln_w]
    un = ATTN_UNROLL
    scratch = [pltpu.VMEM((un, 2, t, t), F32), pltpu.VMEM((un, 2, t, t), BF16),
               pltpu.VMEM((un, 2, 1, t), F32), pltpu.VMEM((un, 2, 1, t), F32),
               pltpu.VMEM((2, nq, 1, t), F32), pltpu.VMEM((2, nq, acc_rows, t), F32)]
    if diff:
        scratch.append(pltpu.VMEM((2, s, HEAD_LANES), BF16))
    return pl.pallas_call(
        functools.partial(_attn_kernel, diff=diff, lam_init=lam_init, n_near=n_near, n_tiles=n_tiles),
        grid=(b, heads),
        in_specs=in_specs,
        out_specs=pl.BlockSpec((1, s, HEAD_LANES), lambda bi, h: (bi, 0, h)),
        out_shape=jax.ShapeDtypeStruct((b, s, heads * HEAD_LANES), BF16),
        scratch_shapes=scratch,
        compiler_params=pltpu.CompilerParams(vmem_limit_bytes=VMEM_LIMIT),
        name="attn_diff" if diff else "attn_mla",
    )(*args)


def _post_ffn_kernel(o_ref, x_ref, g1_ref, wo_ref, ln1g_ref, ln1b_ref, sc_ref, sh_ref, g2_ref,
                     w1_ref, w2_ref, ln2g_ref, ln2b_ref, out_ref):
    y = _dot(o_ref[0], wo_ref[...])
    z = DEEPNORM_ALPHA * x_ref[0] + (1.0 + g1_ref[...]) * y
    x1 = _layer_norm_rows(z, ln1g_ref[...], ln1b_ref[...])
    u = (x1 * (1.0 + sc_ref[...]) + sh_ref[...]).astype(BF16)
    y = jnp.zeros(x1.shape, F32)
    for c in range(D_FF // FF_CHUNK):
        sl = slice(c * FF_CHUNK, (c + 1) * FF_CHUNK)
        h = jnp.maximum(_dot(u, w1_ref[:, sl]), 0.0)
        y = y + _dot((h * h).astype(BF16), w2_ref[sl, :])
    z = DEEPNORM_ALPHA * x1 + (1.0 + g2_ref[...]) * y
    out_ref[0] = _layer_norm_rows(z, ln2g_ref[...], ln2b_ref[...])


def _post_ffn(o, x, mod, layer, wo, w1, w2, ln_g, ln_b):
    b, s, d = x.shape
    resident = functools.partial(pl.BlockSpec, index_map=lambda bi, i: (0, 0), pipeline_mode=pl.Buffered(1))
    return pl.pallas_call(
        _post_ffn_kernel,
        grid=(b, s // ROW_TILE),
        in_specs=[_row_spec(d), _row_spec(d), _mod_spec(layer, 2), resident((d, d)),
                  _whole_spec((1, d)), _whole_spec((1, d)), _mod_spec(layer, 4), _mod_spec(layer, 3),
                  _mod_spec(layer, 5), resident((d, D_FF)), resident((D_FF, d)),
                  _whole_spec((1, d)), _whole_spec((1, d))],
        out_specs=_row_spec(d),
        out_shape=jax.ShapeDtypeStruct((b, s, d), F32),
        compiler_params=pltpu.CompilerParams(vmem_limit_bytes=VMEM_LIMIT),
        name=f"post_ffn{layer}",
    )(o, x, mod, wo, ln_g[0][None], ln_b[0][None], mod, mod, mod, w1, w2, ln_g[1][None], ln_b[1][None])


def _rope_tables(pos_offset, s):
    half = MLA_ROPE // 2
    pos = pos_offset[:, None] + jnp.arange(s, dtype=jnp.int32)[None, :]
    inv = ROPE_THETA ** (-jnp.arange(half, dtype=F32) / half)
    ang = pos.astype(F32)[..., None] * inv
    cos = jnp.cos(ang)
    sin = jnp.sin(ang)
    b = pos.shape[0]
    ones = jnp.ones((b, s, MLA_NOPE), F32)
    zeros_tail = jnp.zeros((b, s, HEAD_LANES - MLA_NOPE - MLA_ROPE), F32)
    cos_t = jnp.concatenate([ones, cos, cos, zeros_tail], axis=-1)
    sin_t = jnp.concatenate([jnp.zeros_like(ones), -sin, sin, zeros_tail], axis=-1)
    return cos_t, sin_t


def _pad_heads(w, per_head, lead):
    w = w.reshape(lead, MLA_HEADS, per_head)
    w = jnp.pad(w, ((0, 0), (0, 0), (0, HEAD_LANES - per_head)))
    return w.reshape(lead, MLA_HEADS * HEAD_LANES)


def kernel(x, c, pos_offset, ada_w, ada_b, ln_g, ln_b, rel_table, da_w_qkv, da_w_o, da_lam_q1, da_lam_k1, da_lam_q2, da_lam_k2, da_subln_w, mla_w_down, mla_q_norm_w, mla_w_uq, mla_kv_norm_w, mla_w_ukv, mla_w_o, ffn_w1, ffn_w2):
    b, s, d = x.shape
    assert d == D_MODEL and s % ROW_TILE == 0 and ROW_TILE % ATTN_TILE == 0 and ATTN_TILE % CHUNK == 0

    mod = _modulation(c, ada_w, ada_b)

    w_qkv = da_w_qkv[0].astype(BF16)
    q, k, vt = _pre0(x, mod, w_qkv[:, :d], w_qkv[:, d:2 * d], w_qkv[:, 2 * d:].T)
    lam_vecs = jnp.stack([da_lam_q1[0], da_lam_k1[0], da_lam_q2[0], da_lam_k2[0]])
    lam_init = 0.8 - 0.6 * math.exp(-0.3 * 0)
    o = _attention(q, k, vt, _near_bias(rel_table), lam_vecs, da_subln_w[0].reshape(-1, 1),
                   diff=True, lam_init=lam_init)
    x = _post_ffn(o, x, mod, 0, da_w_o[0].astype(BF16), ffn_w1[0].astype(BF16), ffn_w2[0].astype(BF16),
                  ln_g[0], ln_b[0])

    n_lat = MLA_Q_LORA + MLA_KV_LORA
    w_down = mla_w_down[0]
    w_down = jnp.concatenate(
        [w_down[:, :n_lat], jnp.zeros((d, MLA_NOPE), F32), w_down[:, n_lat:],
         jnp.zeros((d, HEAD_LANES - MLA_NOPE - MLA_ROPE), F32)], axis=-1).astype(BF16)
    w_uq = _pad_heads(mla_w_uq[0], MLA_NOPE + MLA_ROPE, MLA_Q_LORA).astype(BF16)
    w_ukv = mla_w_ukv[0].reshape(MLA_KV_LORA, MLA_HEADS, MLA_NOPE + MLA_V)
    w_uk = _pad_heads(w_ukv[..., :MLA_NOPE].reshape(MLA_KV_LORA, -1), MLA_NOPE, MLA_KV_LORA).astype(BF16)
    w_uvt = w_ukv[..., MLA_NOPE:].reshape(MLA_KV_LORA, -1).T.astype(BF16)
    cos_t, sin_t = _rope_tables(pos_offset, s)
    q, k, vt = _pre1(x, mod, w_down, mla_q_norm_w[0][None], mla_kv_norm_w[0][None],
                     w_uq, w_uk, w_uvt, cos_t, sin_t)
    o = _attention(q, k, vt, _near_mask(), diff=False)
    return _post_ffn(o, x, mod, 1, mla_w_o[0].astype(BF16), ffn_w1[1].astype(BF16), ffn_w2[1].astype(BF16),
                     ln_g[1], ln_b[1])
```

```python
import functools
import math

import jax
import jax.numpy as jnp
import numpy as np
from jax import lax
from jax.experimental import pallas as pl
from jax.experimental.pallas import tpu as pltpu

F32 = jnp.float32
BF16 = jnp.bfloat16

D_MODEL = 1024
DEPTH = 2
CHUNK = 64
DA_HEAD_DIM = 64
DA_HEADS = D_MODEL // (2 * DA_HEAD_DIM)
MLA_HEADS = 16
MLA_NOPE = 64
MLA_ROPE = 32
MLA_V = 64
MLA_Q_LORA = 512
MLA_KV_LORA = 256
ROPE_THETA = 10000.0
REL_BUCKETS = 32
REL_MAX_DIST = 128
D_FF = 4 * D_MODEL
DEEPNORM_ALPHA = (2 * DEPTH) ** 0.25
LN_EPS = 1e-5
RMS_EPS = 1e-6

LANES = 128
HEAD_LANES = 128
ATTN_TILE = 256
ATTN_UNROLL = 8
DOT_LEAD = 2
RETIRE_LAG = 3
SUM_ROWS = 16
ROW_TILE = 1024
FF_CHUNK = 1024
LOG2E = math.log2(math.e)
NEG_BIG = -1e30
VMEM_LIMIT = 56 * 1024 * 1024

_NT = (((1,), (1,)), ((), ()))


def _dot(a, b):
    return jnp.dot(a, b, preferred_element_type=F32)


def _dot_nt(a, b):
    return lax.dot_general(a, b, _NT, preferred_element_type=F32)


def _layer_norm_rows(z, g, b):
    mu = jnp.mean(z, axis=-1, keepdims=True)
    zc = z - mu
    var = jnp.mean(zc * zc, axis=-1, keepdims=True)
    return zc * lax.rsqrt(var + LN_EPS) * g + b


def _rms_norm_rows(z, w):
    return z * lax.rsqrt(jnp.mean(z * z, axis=-1, keepdims=True) + RMS_EPS) * w


def _mod_kernel(c_ref, w_ref, b_ref, o_ref):
    c = c_ref[...]
    c_act = c * jax.nn.sigmoid(c)
    o_ref[0, 0] = _dot(c_act.astype(BF16), w_ref[0].astype(BF16)) + b_ref[0, 0]


def _modulation(c, ada_w, ada_b):
    b = c.shape[0]
    rows = 8
    c_pad = jnp.zeros((rows, D_MODEL), F32).at[:b].set(c)
    out = pl.pallas_call(
        _mod_kernel,
        grid=(DEPTH, 6),
        in_specs=[
            pl.BlockSpec((rows, D_MODEL), lambda i, j: (0, 0)),
            pl.BlockSpec((1, D_MODEL, D_MODEL), lambda i, j: (i, 0, j)),
            pl.BlockSpec((1, 1, 1, D_MODEL), lambda i, j: (i, j, 0, 0)),
        ],
        out_specs=pl.BlockSpec((1, 1, rows, D_MODEL), lambda i, j: (i, j, 0, 0)),
        out_shape=jax.ShapeDtypeStruct((DEPTH, 6, rows, D_MODEL), F32),
        compiler_params=pltpu.CompilerParams(vmem_limit_bytes=VMEM_LIMIT),
        name="mod",
    )(c_pad, ada_w, ada_b.reshape(DEPTH, 6, 1, D_MODEL))
    return out[:, :, :b].reshape(DEPTH, 6, b, 1, D_MODEL)


def _mod_spec(layer, idx):
    return pl.BlockSpec((None, None, None, 1, D_MODEL), lambda b, i: (layer, idx, b, 0, 0))


def _row_spec(width):
    return pl.BlockSpec((1, ROW_TILE, width), lambda b, i: (b, i, 0))


def _whole_spec(shape):
    return pl.BlockSpec(shape, lambda b, i: (0,) * len(shape))


def _vt_rows(maps_per_head):
    return HEAD_LANES + maps_per_head * SUM_ROWS


def _vt_shape(b, s, maps_per_head):
    return (b, DA_HEADS, s // ATTN_TILE, _vt_rows(maps_per_head), ATTN_TILE)


def _vt_out_spec(maps_per_head):
    return pl.BlockSpec((1, DA_HEADS, ROW_TILE // ATTN_TILE, _vt_rows(maps_per_head), ATTN_TILE),
                        lambda b, i: (b, 0, i, 0, 0))


def _store_vt(vt_ref, vt, maps_per_head):
    groups = DA_HEADS * maps_per_head
    vt = vt.astype(BF16).reshape(groups, HEAD_LANES // maps_per_head, ROW_TILE)
    ones = jnp.ones((groups, SUM_ROWS, ROW_TILE), BF16)
    vt = jnp.concatenate([vt, ones], axis=1).reshape(DA_HEADS, _vt_rows(maps_per_head), ROW_TILE)
    for t in range(ROW_TILE // ATTN_TILE):
        vt_ref[0, :, t] = vt[:, :, t * ATTN_TILE:(t + 1) * ATTN_TILE]


def _pre0_kernel(x_ref, sc_ref, sh_ref, wq_ref, wk_ref, wvt_ref, q_ref, k_ref, vt_ref, *, q_scale):
    u = (x_ref[0] * (1.0 + sc_ref[...]) + sh_ref[...]).astype(BF16)
    q_ref[0] = (_dot(u, wq_ref[...]) * q_scale).astype(BF16)
    k_ref[0] = _dot(u, wk_ref[...]).astype(BF16)
    _store_vt(vt_ref, _dot_nt(wvt_ref[...], u), 1)


def _pre0(x, mod, wq, wk, wvt):
    b, s, d = x.shape
    q_scale = DA_HEAD_DIM ** -0.5 * LOG2E
    return pl.pallas_call(
        functools.partial(_pre0_kernel, q_scale=q_scale),
        grid=(b, s // ROW_TILE),
        in_specs=[_row_spec(d), _mod_spec(0, 1), _mod_spec(0, 0),
                  _whole_spec((d, d)), _whole_spec((d, d)), _whole_spec((d, d))],
        out_specs=[_row_spec(d), _row_spec(d), _vt_out_spec(1)],
        out_shape=[jax.ShapeDtypeStruct((b, s, d), BF16),
                   jax.ShapeDtypeStruct((b, s, d), BF16),
                   jax.ShapeDtypeStruct(_vt_shape(b, s, 1), BF16)],
        compiler_params=pltpu.CompilerParams(vmem_limit_bytes=VMEM_LIMIT),
        name="pre0",
    )(x, mod, mod, wq, wk, wvt)


def _pre1_kernel(x_ref, sc_ref, sh_ref, wd_ref, qn_ref, kvn_ref, wuq_ref, wuk_ref, wuvt_ref,
                 cos_ref, sin_ref, q_ref, k_ref, vt_ref, *, q_scale):
    u = (x_ref[0] * (1.0 + sc_ref[...]) + sh_ref[...]).astype(BF16)
    down = _dot(u, wd_ref[...])
    c_q = _rms_norm_rows(down[:, :MLA_Q_LORA], qn_ref[...]).astype(BF16)
    c_kv = _rms_norm_rows(down[:, MLA_Q_LORA:MLA_Q_LORA + MLA_KV_LORA], kvn_ref[...]).astype(BF16)
    k_rope = down[:, MLA_Q_LORA + MLA_KV_LORA:]
    q = _dot(c_q, wuq_ref[...])
    k_nope = _dot(c_kv, wuk_ref[...])
    _store_vt(vt_ref, _dot_nt(wuvt_ref[...], c_kv), 2)

    cos = cos_ref[0]
    sin = sin_ref[0]
    lane = lax.broadcasted_iota(jnp.int32, cos.shape, 1)
    first_half = lane < MLA_NOPE + MLA_ROPE // 2

    def rope(xh):
        partner = jnp.where(first_half,
                            pltpu.roll(xh, HEAD_LANES - MLA_ROPE // 2, 1),
                            pltpu.roll(xh, MLA_ROPE // 2, 1))
        return xh * cos + partner * sin

    k_rope = rope(k_rope)
    for h in range(MLA_HEADS):
        sl = slice(h * HEAD_LANES, (h + 1) * HEAD_LANES)
        q_ref[0, :, sl] = (rope(q[:, sl]) * q_scale).astype(BF16)
        k_ref[0, :, sl] = (k_nope[:, sl] + k_rope).astype(BF16)


def _pre1(x, mod, wd, qn, kvn, wuq, wuk, wuvt, cos_t, sin_t):
    b, s, d = x.shape
    hw = MLA_HEADS * HEAD_LANES
    q_scale = (MLA_NOPE + MLA_ROPE) ** -0.5 * LOG2E
    return pl.pallas_call(
        functools.partial(_pre1_kernel, q_scale=q_scale),
        grid=(b, s // ROW_TILE),
        in_specs=[_row_spec(d), _mod_spec(1, 1), _mod_spec(1, 0),
                  _whole_spec(wd.shape), _whole_spec(qn.shape), _whole_spec(kvn.shape),
                  _whole_spec(wuq.shape), _whole_spec(wuk.shape), _whole_spec(wuvt.shape),
                  _row_spec(HEAD_LANES), _row_spec(HEAD_LANES)],
        out_specs=[_row_spec(hw), _row_spec(hw), _vt_out_spec(2)],
        out_shape=[jax.ShapeDtypeStruct((b, s, hw), BF16),
                   jax.ShapeDtypeStruct((b, s, hw), BF16),
                   jax.ShapeDtypeStruct(_vt_shape(b, s, 2), BF16)],
        compiler_params=pltpu.CompilerParams(vmem_limit_bytes=VMEM_LIMIT),
        name="pre1",
    )(x, mod, mod, wd, qn, kvn, wuq, wuk, wuvt, cos_t, sin_t)


def _t5_bucket(rel):
    nb = REL_BUCKETS // 2
    max_exact = nb // 2
    n_log = nb - max_exact
    thresholds = []
    for k in range(1, n_log):
        n = max_exact
        while n ** n_log * max_exact ** k < REL_MAX_DIST ** k * max_exact ** n_log:
            n += 1
        thresholds.append(n)
    n = np.abs(rel)
    large = max_exact + sum((n >= thr).astype(np.int32) for thr in thresholds)
    return (rel > 0).astype(np.int32) * nb + np.where(n < max_exact, n, large)


def _near_bucket_tiles():
    kk = np.arange(ATTN_TILE, dtype=np.int32)[:, None]
    qq = np.arange(ATTN_TILE, dtype=np.int32)[None, :]
    diag = np.where(kk // CHUNK <= qq // CHUNK, _t5_bucket(kk - qq), -1)
    prev = _t5_bucket(kk - ATTN_TILE - qq)
    return jnp.asarray(np.stack([diag, prev]).astype(np.int32))


def _bias_kernel(tab_ref, idx_ref, o_ref):
    h = pl.program_id(0)
    idx = idx_ref[...]
    val = jnp.zeros(idx.shape, F32)
    for bkt in range(REL_BUCKETS):
        val = jnp.where(idx == bkt, tab_ref[bkt, h], val)
    far = tab_ref[REL_BUCKETS // 2 - 1, h]
    o_ref[0] = jnp.where(idx < 0, NEG_BIG, (val - far) * LOG2E)


def _near_bias(rel_table):
    idx = _near_bucket_tiles()
    return pl.pallas_call(
        _bias_kernel,
        grid=(DA_HEADS,),
        in_specs=[pl.BlockSpec(memory_space=pltpu.SMEM),
                  pl.BlockSpec(idx.shape, lambda h: (0, 0, 0))],
        out_specs=pl.BlockSpec((1,) + idx.shape, lambda h: (h, 0, 0, 0)),
        out_shape=jax.ShapeDtypeStruct((DA_HEADS,) + idx.shape, F32),
        name="near_bias",
    )(rel_table, idx)


def _near_mask():
    kk = jnp.arange(ATTN_TILE, dtype=jnp.int32)[:, None]
    qq = jnp.arange(ATTN_TILE, dtype=jnp.int32)[None, :]
    return jnp.where(kk // CHUNK <= qq // CHUNK, 0.0, NEG_BIG).astype(F32)[None, None]


def _tile_tables(nq, n_near_kinds):
    ti, tj, ta = [], [], []
    for kind in range(n_near_kinds):
        for i in range(kind, nq):
            ti.append(i), tj.append(i - kind), ta.append(kind)
    n_near = len(ti)
    for j in range(nq - n_near_kinds):
        for i in range(j + n_near_kinds, nq):
            ti.append(i), tj.append(j), ta.append(0)
    return np.asarray(ti, np.int32), np.asarray(tj, np.int32), np.asarray(ta, np.int32), n_near


def _attn_kernel(*refs, diff, lam_init, n_near, n_tiles):
    ti_ref, tj_ref, ta_ref, q_ref, k_ref, vt_ref, near_ref = refs[:7]
    refs = refs[7:]
    if diff:
        lam_ref, subw_ref = refs[:2]
        refs = refs[2:]
    o_ref, s_buf, p_buf, mx_buf, al_buf, m_scr, acc = refs[:7]
    t = ATTN_TILE
    unroll = s_buf.shape[0]
    nq = acc.shape[1]
    acc_rows = acc.shape[2]
    dv = acc_rows - SUM_ROWS

    if diff:
        qm = refs[7]
        q_all = q_ref[0]
        lane = lax.broadcasted_iota(jnp.int32, q_all.shape, 1)
        zero = jnp.zeros_like(q_all)
        qm[0] = jnp.where(lane < DA_HEAD_DIM, q_all, zero)
        qm[1] = jnp.where(lane >= DA_HEAD_DIM, q_all, zero)

    m_scr[...] = jnp.full(m_scr.shape, NEG_BIG, F32)
    acc[...] = jnp.zeros(acc.shape, F32)

    def rows(idx):
        return pl.ds(pl.multiple_of(idx * t, t), t)

    def q_map(mp, i):
        if diff:
            return qm[mp, rows(i), :]
        return q_ref[0, rows(i), mp * HEAD_LANES:(mp + 1) * HEAD_LANES]

    def score_dot(n, mp):
        i, j = ti_ref[n], tj_ref[n]
        k_m = k_ref[0, rows(j), :] if diff else k_ref[0, rows(j), mp * HEAD_LANES:(mp + 1) * HEAD_LANES]
        return _dot_nt(k_m, q_map(mp, i))

    def score_store(n, u, mp, s, near):
        s_buf[u, mp] = s
        if near:
            s = s_buf[u, mp] + near_ref[0, ta_ref[n]]
            s_buf[u, mp] = s
        mx_buf[u, mp] = jnp.max(s, axis=0, keepdims=True)

    def exps(n, u, mp):
        i = ti_ref[n]
        m_old = m_scr[mp, i]
        m_new = jnp.maximum(m_old, mx_buf[u, mp])
        alpha = jnp.exp2(m_old - m_new)
        p = jnp.exp2(s_buf[u, mp] - m_new)
        m_scr[mp, i] = m_new
        p_buf[u, mp] = p.astype(BF16)
        al_buf[u, mp] = alpha

    def pv_dot(n, u, mp):
        j = tj_ref[n]
        vt_m = vt_ref[0, 0, j] if diff else vt_ref[0, 0, j, mp * acc_rows:(mp + 1) * acc_rows, :]
        return _dot(vt_m, p_buf[u, mp]), al_buf[u, mp]

    def acc_update(n, mp, pv, alpha):
        i = ti_ref[n]
        acc[mp, i] = alpha * acc[mp, i] + pv

    items = [(u, mp) for u in range(unroll) for mp in range(2)]

    def step(trip, near_flags, has1=True, has2=True, has3=True):
        pending = {}

        def issue(kk):
            u, mp = items[kk]
            pv = pv_dot(unroll * (trip - 2) + u, u, mp) if has3 else None
            s = score_dot(unroll * trip + u, mp) if has1 else None
            pending[kk] = (s, pv)

        def retire(kk):
            u, mp = items[kk]
            s, pv = pending.pop(kk)
            if has1:
                score_store(unroll * trip + u, u, mp, s, near_flags[u])
            if has3:
                acc_update(unroll * (trip - 2) + u, mp, *pv)

        for kk in range(min(DOT_LEAD, len(items))):
            issue(kk)
        for kk, (u, mp) in enumerate(items):
            if has2:
                exps(unroll * (trip - 1) + u, u, mp)
            if kk + DOT_LEAD < len(items):
                issue(kk + DOT_LEAD)
            if kk + DOT_LEAD - RETIRE_LAG >= 0:
                retire(kk + DOT_LEAD - RETIRE_LAG)
        for kk in sorted(pending):
            retire(kk)

    n_trips = n_tiles // unroll

    def near_flags(trip):
        return [unroll * trip + u < n_near for u in range(unroll)]

    def uniform(trip):
        flags = near_flags(trip)
        return all(flags) or not any(flags)

    step(0, near_flags(0), has2=False, has3=False)
    step(1, near_flags(1), has3=False)
    trip = 2
    while trip < n_trips:
        end = trip + 1
        while uniform(trip) and end < n_trips and near_flags(end) == near_flags(trip):
            end += 1
        if end - trip >= 2:
            flags = near_flags(trip)
            lax.fori_loop(trip, end, lambda tt, carry: (step(tt, flags), carry)[1], 0, unroll=2)
        else:
            step(trip, near_flags(trip))
        trip = end
    step(n_trips, None, has1=False)
    step(n_trips + 1, None, has1=False, has2=False)

    if diff:
        lam = (jnp.exp(jnp.sum(lam_ref[0:1] * lam_ref[1:2], axis=-1, keepdims=True))
               - jnp.exp(jnp.sum(lam_ref[2:3] * lam_ref[3:4], axis=-1, keepdims=True)) + lam_init)
        sub_w = subw_ref[...] * (1.0 - lam_init)

    def finalize(i, carry):
        o_a = acc[0, i, :dv] / acc[0, i, dv:dv + 1]
        o_b = acc[1, i, :dv] / acc[1, i, dv:dv + 1]
        if diff:
            o = o_a - lam * o_b
            o = o * lax.rsqrt(jnp.mean(o * o, axis=0, keepdims=True) + RMS_EPS) * sub_w
        else:
            o = jnp.concatenate([o_a, o_b], axis=0)
        o_ref[0, rows(i), :] = o.T.astype(BF16)
        return carry

    lax.fori_loop(0, nq, finalize, 0, unroll=4)


def _attention(q, k, vt, near, lam_vecs=None, subln_w=None, *, diff, lam_init=0.0):
    b, s, _ = q.shape
    t = ATTN_TILE
    nq = s // t
    heads = DA_HEADS
    width = HEAD_LANES if diff else 2 * HEAD_LANES
    acc_rows = (HEAD_LANES if diff else MLA_V) + SUM_ROWS
    assert vt.shape[3] == (1 if diff else 2) * acc_rows
    near_heads = near.shape[0]
    ti, tj, ta, n_near = _tile_tables(nq, near.shape[1])
    n_tiles = len(ti)
    assert n_tiles % ATTN_UNROLL == 0 and n_tiles // ATTN_UNROLL >= 2
    smem = pl.BlockSpec(memory_space=pltpu.SMEM)
    in_specs = [
        smem, smem, smem,
        pl.BlockSpec((1, s, width), lambda bi, h: (bi, 0, h)),
        pl.BlockSpec((1, s, width), lambda bi, h: (bi, 0, h)),
        pl.BlockSpec((1, 1, nq, vt.shape[3], t), lambda bi, h: (bi, h, 0, 0, 0)),
        pl.BlockSpec((1,) + near.shape[1:], lambda bi, h: (h if near_heads > 1 else 0, 0, 0, 0)),
    ]
    args = [jnp.asarray(ti), jnp.asarray(tj), jnp.asarray(ta), q, k, vt, near]
    if diff:
        in_specs += [pl.BlockSpec(lam_vecs.shape, lambda bi, h: (0, 0)),
                     pl.BlockSpec(subln_w.shape, lambda bi, h: (0, 0))]
        args += [lam_vecs, subln_w]
    un = ATTN_UNROLL
    scratch = [pltpu.VMEM((un, 2, t, t), F32), pltpu.VMEM((un, 2, t, t), BF16),
               pltpu.VMEM((un, 2, 1, t), F32), pltpu.VMEM((un, 2, 1, t), F32),
               pltpu.VMEM((2, nq, 1, t), F32), pltpu.VMEM((2, nq, acc_rows, t), F32)]
    if diff:
        scratch.append(pltpu.VMEM((2, s, HEAD_LANES), BF16))
    return pl.pallas_call(
        functools.partial(_attn_kernel, diff=diff, lam_init=lam_init, n_near=n_near, n_tiles=n_tiles),
        grid=(b, heads),
        in_specs=in_specs,
        out_specs=pl.BlockSpec((1, s, HEAD_LANES), lambda bi, h: (bi, 0, h)),
        out_shape=jax.ShapeDtypeStruct((b, s, heads * HEAD_LANES), BF16),
        scratch_shapes=scratch,
        compiler_params=pltpu.CompilerParams(vmem_limit_bytes=VMEM_LIMIT),
        name="attn_diff" if diff else "attn_mla",
    )(*args)


def _post_ffn_kernel(o_ref, x_ref, g1_ref, wo_ref, ln1g_ref, ln1b_ref, sc_ref, sh_ref, g2_ref,
                     w1_ref, w2_ref, ln2g_ref, ln2b_ref, out_ref):
    y = _dot(o_ref[0], wo_ref[...])
    z = DEEPNORM_ALPHA * x_ref[0] + (1.0 + g1_ref[...]) * y
    x1 = _layer_norm_rows(z, ln1g_ref[...], ln1b_ref[...])
    u = (x1 * (1.0 + sc_ref[...]) + sh_ref[...]).astype(BF16)
    y = jnp.zeros(x1.shape, F32)
    for c in range(D_FF // FF_CHUNK):
        sl = slice(c * FF_CHUNK, (c + 1) * FF_CHUNK)
        h = jnp.maximum(_dot(u, w1_ref[:, sl]), 0.0)
        y = y + _dot((h * h).astype(BF16), w2_ref[sl, :])
    z = DEEPNORM_ALPHA * x1 + (1.0 + g2_ref[...]) * y
    out_ref[0] = _layer_norm_rows(z, ln2g_ref[...], ln2b_ref[...])


def _post_ffn(o, x, mod, layer, wo, w1, w2, ln_g, ln_b):
    b, s, d = x.shape
    resident = functools.partial(pl.BlockSpec, index_map=lambda bi, i: (0, 0), pipeline_mode=pl.Buffered(1))
    return pl.pallas_call(
        _post_ffn_kernel,
        grid=(b, s // ROW_TILE),
        in_specs=[_row_spec(d), _row_spec(d), _mod_spec(layer, 2), resident((d, d)),
                  _whole_spec((1, d)), _whole_spec((1, d)), _mod_spec(layer, 4), _mod_spec(layer, 3),
                  _mod_spec(layer, 5), resident((d, D_FF)), resident((D_FF, d)),
                  _whole_spec((1, d)), _whole_spec((1, d))],
        out_specs=_row_spec(d),
        out_shape=jax.ShapeDtypeStruct((b, s, d), F32),
        compiler_params=pltpu.CompilerParams(vmem_limit_bytes=VMEM_LIMIT),
        name=f"post_ffn{layer}",
    )(o, x, mod, wo, ln_g[0][None], ln_b[0][None], mod, mod, mod, w1, w2, ln_g[1][None], ln_b[1][None])


def _rope_tables(pos_offset, s):
    half = MLA_ROPE // 2
    pos = pos_offset[:, None] + jnp.arange(s, dtype=jnp.int32)[None, :]
    inv = ROPE_THETA ** (-jnp.arange(half, dtype=F32) / half)
    ang = pos.astype(F32)[..., None] * inv
    cos = jnp.cos(ang)
    sin = jnp.sin(ang)
    b = pos.shape[0]
    ones = jnp.ones((b, s, MLA_NOPE), F32)
    zeros_tail = jnp.zeros((b, s, HEAD_LANES - MLA_NOPE - MLA_ROPE), F32)
    cos_t = jnp.concatenate([ones, cos, cos, zeros_tail], axis=-1)
    sin_t = jnp.concatenate([jnp.zeros_like(ones), -sin, sin, zeros_tail], axis=-1)
    return cos_t, sin_t


def _pad_heads(w, per_head, lead):
    w = w.reshape(lead, MLA_HEADS, per_head)
    w = jnp.pad(w, ((0, 0), (0, 0), (0, HEAD_LANES - per_head)))
    return w.reshape(lead, MLA_HEADS * HEAD_LANES)


def kernel(x, c, pos_offset, ada_w, ada_b, ln_g, ln_b, rel_table, da_w_qkv, da_w_o, da_lam_q1, da_lam_k1, da_lam_q2, da_lam_k2, da_subln_w, mla_w_down, mla_q_norm_w, mla_w_uq, mla_kv_norm_w, mla_w_ukv, mla_w_o, ffn_w1, ffn_w2):
    b, s, d = x.shape
    assert d == D_MODEL and s % ROW_TILE == 0 and ROW_TILE % ATTN_TILE == 0 and ATTN_TILE % CHUNK == 0

    mod = _modulation(c, ada_w, ada_b)

    w_qkv = da_w_qkv[0].astype(BF16)
    q, k, vt = _pre0(x, mod, w_qkv[:, :d], w_qkv[:, d:2 * d], w_qkv[:, 2 * d:].T)
    lam_vecs = jnp.stack([da_lam_q1[0], da_lam_k1[0], da_lam_q2[0], da_lam_k2[0]])
    lam_init = 0.8 - 0.6 * math.exp(-0.3 * 0)
    o = _attention(q, k, vt, _near_bias(rel_table), lam_vecs, da_subln_w[0].reshape(-1, 1),
                   diff=True, lam_init=lam_init)
    x = _post_ffn(o, x, mod, 0, da_w_o[0].astype(BF16), ffn_w1[0].astype(BF16), ffn_w2[0].astype(BF16),
                  ln_g[0], ln_b[0])

    n_lat = MLA_Q_LORA + MLA_KV_LORA
    w_down = mla_w_down[0]
    w_down = jnp.concatenate(
        [w_down[:, :n_lat], jnp.zeros((d, MLA_NOPE), F32), w_down[:, n_lat:],
         jnp.zeros((d, HEAD_LANES - MLA_NOPE - MLA_ROPE), F32)], axis=-1).astype(BF16)
    w_uq = _pad_heads(mla_w_uq[0], MLA_NOPE + MLA_ROPE, MLA_Q_LORA).astype(BF16)
    w_ukv = mla_w_ukv[0].reshape(MLA_KV_LORA, MLA_HEADS, MLA_NOPE + MLA_V)
    w_uk = _pad_heads(w_ukv[..., :MLA_NOPE].reshape(MLA_KV_LORA, -1), MLA_NOPE, MLA_KV_LORA).astype(BF16)
    w_uvt = w_ukv[..., MLA_NOPE:].reshape(MLA_KV_LORA, -1).T.astype(BF16)
    cos_t, sin_t = _rope_tables(pos_offset, s)
    q, k, vt = _pre1(x, mod, w_down, mla_q_norm_w[0][None], mla_kv_norm_w[0][None],
                     w_uq, w_uk, w_uvt, cos_t, sin_t)
    o = _attention(q, k, vt, _near_mask(), diff=False)
    return _post_ffn(o, x, mod, 1, mla_w_o[0].astype(BF16), ffn_w1[1].astype(BF16), ffn_w2[1].astype(BF16),
                     ln_g[1], ln_b[1])
```

```python
import functools
import math

import jax
import jax.numpy as jnp
import numpy as np
from jax import lax
from jax.experimental import pallas as pl
from jax.experimental.pallas import tpu as pltpu

F32 = jnp.float32
BF16 = jnp.bfloat16

D_MODEL = 1024
DEPTH = 2
CHUNK = 64
DA_HEAD_DIM = 64
DA_HEADS = D_MODEL // (2 * DA_HEAD_DIM)
MLA_HEADS = 16
MLA_NOPE = 64
MLA_ROPE = 32
MLA_V = 64
MLA_Q_LORA = 512
MLA_KV_LORA = 256
ROPE_THETA = 10000.0
REL_BUCKETS = 32
REL_MAX_DIST = 128
D_FF = 4 * D_MODEL
DEEPNORM_ALPHA = (2 * DEPTH) ** 0.25
LN_EPS = 1e-5
RMS_EPS = 1e-6

LANES = 128
SUBLANES = 8
HEAD_LANES = 128
ATTN_TILE = 256
ATTN_UNROLL = 8
ATTN_LOOP_UNROLL = 1
DOT_LEAD = 2
RETIRE_LAG = 3
SUM_ROWS = 16
ROW_TILE = 1024
FF_CHUNK = 1024
LOG2E = math.log2(math.e)
NEG_BIG = -1e30
VMEM_LIMIT = 56 * 1024 * 1024

_NT = (((1,), (1,)), ((), ()))


def _dot(a, b):
    return jnp.dot(a, b, preferred_element_type=F32)


def _dot_nt(a, b):
    return lax.dot_general(a, b, _NT, preferred_element_type=F32)


def _layer_norm_rows(z, g, b):
    mu = jnp.mean(z, axis=-1, keepdims=True)
    zc = z - mu
    var = jnp.mean(zc * zc, axis=-1, keepdims=True)
    return zc * lax.rsqrt(var + LN_EPS) * g + b


def _rms_norm_rows(z, w):
    return z * lax.rsqrt(jnp.mean(z * z, axis=-1, keepdims=True) + RMS_EPS) * w


def _mod_kernel(c_ref, w_ref, b_ref, o_ref):
    c = c_ref[...]
    c_act = c * jax.nn.sigmoid(c)
    o_ref[0, 0] = _dot(c_act.astype(BF16), w_ref[0].astype(BF16)) + b_ref[0, 0]


def _modulation(c, ada_w, ada_b):
    b = c.shape[0]
    rows = 8
    c_pad = jnp.zeros((rows, D_MODEL), F32).at[:b].set(c)
    out = pl.pallas_call(
        _mod_kernel,
        grid=(DEPTH, 6),
        in_specs=[
            pl.BlockSpec((rows, D_MODEL), lambda i, j: (0, 0)),
            pl.BlockSpec((1, D_MODEL, D_MODEL), lambda i, j: (i, 0, j)),
            pl.BlockSpec((1, 1, 1, D_MODEL), lambda i, j: (i, j, 0, 0)),
        ],
        out_specs=pl.BlockSpec((1, 1, rows, D_MODEL), lambda i, j: (i, j, 0, 0)),
        out_shape=jax.ShapeDtypeStruct((DEPTH, 6, rows, D_MODEL), F32),
        compiler_params=pltpu.CompilerParams(vmem_limit_bytes=VMEM_LIMIT),
        name="mod",
    )(c_pad, ada_w, ada_b.reshape(DEPTH, 6, 1, D_MODEL))
    return out[:, :, :b].reshape(DEPTH, 6, b, 1, D_MODEL)


def _mod_spec(layer, idx):
    return pl.BlockSpec((None, None, None, 1, D_MODEL), lambda b, i: (layer, idx, b, 0, 0))


def _row_spec(width):
    return pl.BlockSpec((1, ROW_TILE, width), lambda b, i: (b, i, 0))


def _whole_spec(shape):
    return pl.BlockSpec(shape, lambda b, i: (0,) * len(shape))


def _vt_rows(maps_per_head):
    return HEAD_LANES + maps_per_head * SUM_ROWS


def _vt_shape(b, s, maps_per_head):
    return (b, DA_HEADS, s // ATTN_TILE, _vt_rows(maps_per_head), ATTN_TILE)


def _vt_out_spec(maps_per_head):
    return pl.BlockSpec((1, DA_HEADS, ROW_TILE // ATTN_TILE, _vt_rows(maps_per_head), ATTN_TILE),
                        lambda b, i: (b, 0, i, 0, 0))


def _store_vt(vt_ref, vt, maps_per_head):
    groups = DA_HEADS * maps_per_head
    vt = vt.astype(BF16).reshape(groups, HEAD_LANES // maps_per_head, ROW_TILE)
    ones = jnp.ones((groups, SUM_ROWS, ROW_TILE), BF16)
    vt = jnp.concatenate([vt, ones], axis=1).reshape(DA_HEADS, _vt_rows(maps_per_head), ROW_TILE)
    for t in range(ROW_TILE // ATTN_TILE):
        vt_ref[0, :, t] = vt[:, :, t * ATTN_TILE:(t + 1) * ATTN_TILE]


def _pre0_kernel(x_ref, sc_ref, sh_ref, wq_ref, wk_ref, wvt_ref, q_ref, k_ref, vt_ref, *, q_scale):
    u = (x_ref[0] * (1.0 + sc_ref[...]) + sh_ref[...]).astype(BF16)
    q_ref[0] = (_dot(u, wq_ref[...]) * q_scale).astype(BF16)
    k_ref[0] = _dot(u, wk_ref[...]).astype(BF16)
    _store_vt(vt_ref, _dot_nt(wvt_ref[...], u), 1)


def _pre0(x, mod, wq, wk, wvt):
    b, s, d = x.shape
    q_scale = DA_HEAD_DIM ** -0.5 * LOG2E
    return pl.pallas_call(
        functools.partial(_pre0_kernel, q_scale=q_scale),
        grid=(b, s // ROW_TILE),
        in_specs=[_row_spec(d), _mod_spec(0, 1), _mod_spec(0, 0),
                  _whole_spec((d, d)), _whole_spec((d, d)), _whole_spec((d, d))],
        out_specs=[_row_spec(d), _row_spec(d), _vt_out_spec(1)],
        out_shape=[jax.ShapeDtypeStruct((b, s, d), BF16),
                   jax.ShapeDtypeStruct((b, s, d), BF16),
                   jax.ShapeDtypeStruct(_vt_shape(b, s, 1), BF16)],
        compiler_params=pltpu.CompilerParams(vmem_limit_bytes=VMEM_LIMIT),
        name="pre0",
    )(x, mod, mod, wq, wk, wvt)


def _pre1_kernel(x_ref, sc_ref, sh_ref, wd_ref, qn_ref, kvn_ref, wuq_ref, wuk_ref, wuvt_ref,
                 cos_ref, sin_ref, q_ref, k_ref, vt_ref, *, q_scale):
    u = (x_ref[0] * (1.0 + sc_ref[...]) + sh_ref[...]).astype(BF16)
    down = _dot(u, wd_ref[...])
    c_q = _rms_norm_rows(down[:, :MLA_Q_LORA], qn_ref[...]).astype(BF16)
    c_kv = _rms_norm_rows(down[:, MLA_Q_LORA:MLA_Q_LORA + MLA_KV_LORA], kvn_ref[...]).astype(BF16)
    k_rope = down[:, MLA_Q_LORA + MLA_KV_LORA:]
    q = _dot(c_q, wuq_ref[...])
    k_nope = _dot(c_kv, wuk_ref[...])
    _store_vt(vt_ref, _dot_nt(wuvt_ref[...], c_kv), 2)

    cos = cos_ref[0]
    sin = sin_ref[0]
    lane = lax.broadcasted_iota(jnp.int32, cos.shape, 1)
    first_half = lane < MLA_NOPE + MLA_ROPE // 2

    def rope(xh):
        partner = jnp.where(first_half,
                            pltpu.roll(xh, HEAD_LANES - MLA_ROPE // 2, 1),
                            pltpu.roll(xh, MLA_ROPE // 2, 1))
        return xh * cos + partner * sin

    k_rope = rope(k_rope)
    for h in range(MLA_HEADS):
        sl = slice(h * HEAD_LANES, (h + 1) * HEAD_LANES)
        q_ref[0, :, sl] = (rope(q[:, sl]) * q_scale).astype(BF16)
        k_ref[0, :, sl] = (k_nope[:, sl] + k_rope).astype(BF16)


def _pre1(x, mod, wd, qn, kvn, wuq, wuk, wuvt, cos_t, sin_t):
    b, s, d = x.shape
    hw = MLA_HEADS * HEAD_LANES
    q_scale = (MLA_NOPE + MLA_ROPE) ** -0.5 * LOG2E
    return pl.pallas_call(
        functools.partial(_pre1_kernel, q_scale=q_scale),
        grid=(b, s // ROW_TILE),
        in_specs=[_row_spec(d), _mod_spec(1, 1), _mod_spec(1, 0),
                  _whole_spec(wd.shape), _whole_spec(qn.shape), _whole_spec(kvn.shape),
                  _whole_spec(wuq.shape), _whole_spec(wuk.shape), _whole_spec(wuvt.shape),
                  _row_spec(HEAD_LANES), _row_spec(HEAD_LANES)],
        out_specs=[_row_spec(hw), _row_spec(hw), _vt_out_spec(2)],
        out_shape=[jax.ShapeDtypeStruct((b, s, hw), BF16),
                   jax.ShapeDtypeStruct((b, s, hw), BF16),
                   jax.ShapeDtypeStruct(_vt_shape(b, s, 2), BF16)],
        compiler_params=pltpu.CompilerParams(vmem_limit_bytes=VMEM_LIMIT),
        name="pre1",
    )(x, mod, mod, wd, qn, kvn, wuq, wuk, wuvt, cos_t, sin_t)


def _t5_bucket(rel):
    nb = REL_BUCKETS // 2
    max_exact = nb // 2
    n_log = nb - max_exact
    thresholds = []
    for k in range(1, n_log):
        n = max_exact
        while n ** n_log * max_exact ** k < REL_MAX_DIST ** k * max_exact ** n_log:
            n += 1
        thresholds.append(n)
    n = np.abs(rel)
    large = max_exact + sum((n >= thr).astype(np.int32) for thr in thresholds)
    return (rel > 0).astype(np.int32) * nb + np.where(n < max_exact, n, large)


def _near_bucket_tiles():
    kk = np.arange(ATTN_TILE, dtype=np.int32)[:, None]
    qq = np.arange(ATTN_TILE, dtype=np.int32)[None, :]
    diag = np.where(kk // CHUNK <= qq // CHUNK, _t5_bucket(kk - qq), -1)
    prev = _t5_bucket(kk - ATTN_TILE - qq)
    return jnp.asarray(np.stack([diag, prev]).astype(np.int32))


def _bias_kernel(tab_ref, idx_ref, o_ref):
    h = pl.program_id(0)
    idx = idx_ref[...]
    val = jnp.zeros(idx.shape, F32)
    for bkt in range(REL_BUCKETS):
        val = jnp.where(idx == bkt, tab_ref[bkt, h], val)
    far = tab_ref[REL_BUCKETS // 2 - 1, h]
    o_ref[0] = jnp.where(idx < 0, NEG_BIG, (val - far) * LOG2E)


def _near_bias(rel_table):
    idx = _near_bucket_tiles()
    return pl.pallas_call(
        _bias_kernel,
        grid=(DA_HEADS,),
        in_specs=[pl.BlockSpec(memory_space=pltpu.SMEM),
                  pl.BlockSpec(idx.shape, lambda h: (0, 0, 0))],
        out_specs=pl.BlockSpec((1,) + idx.shape, lambda h: (h, 0, 0, 0)),
        out_shape=jax.ShapeDtypeStruct((DA_HEADS,) + idx.shape, F32),
        name="near_bias",
    )(rel_table, idx)


def _near_mask():
    kk = jnp.arange(ATTN_TILE, dtype=jnp.int32)[:, None]
    qq = jnp.arange(ATTN_TILE, dtype=jnp.int32)[None, :]
    return jnp.where(kk // CHUNK <= qq // CHUNK, 0.0, NEG_BIG).astype(F32)[None, None]


def _tile_tables(nq, n_near_kinds):
    ti, tj, ta = [], [], []
    for kind in range(n_near_kinds):
        for i in range(kind, nq):
            ti.append(i), tj.append(i - kind), ta.append(kind)
    n_near = len(ti)
    for j in range(nq - n_near_kinds):
        for i in range(j + n_near_kinds, nq):
            ti.append(i), tj.append(j), ta.append(0)
    return np.asarray(ti, np.int32), np.asarray(tj, np.int32), np.asarray(ta, np.int32), n_near


def _attn_kernel(*refs, diff, lam_init, n_near, n_tiles):
    ti_ref, tj_ref, ta_ref, q_ref, k_ref, vt_ref, near_ref = refs[:7]
    refs = refs[7:]
    if diff:
        lam_ref, subw_ref = refs[:2]
        refs = refs[2:]
    o_ref, s_buf, p_buf, mx_buf, al_buf, m_scr, acc = refs[:7]
    t = ATTN_TILE
    unroll = s_buf.shape[0]
    nq = acc.shape[1]
    acc_rows = acc.shape[2]
    dv = acc_rows - SUM_ROWS

    if diff:
        qm = refs[7]
        q_all = q_ref[0]
        lane = lax.broadcasted_iota(jnp.int32, q_all.shape, 1)
        zero = jnp.zeros_like(q_all)
        qm[0] = jnp.where(lane < DA_HEAD_DIM, q_all, zero)
        qm[1] = jnp.where(lane >= DA_HEAD_DIM, q_all, zero)

    m_scr[...] = jnp.full(m_scr.shape, NEG_BIG, F32)
    acc[...] = jnp.zeros(acc.shape, F32)

    def rows(idx):
        return pl.ds(pl.multiple_of(idx * t, t), t)

    def q_map(mp, i):
        if diff:
            return qm[mp, rows(i), :]
        return q_ref[0, rows(i), mp * HEAD_LANES:(mp + 1) * HEAD_LANES]

    def score_dot(n, mp):
        i, j = ti_ref[n], tj_ref[n]
        k_m = k_ref[0, rows(j), :] if diff else k_ref[0, rows(j), mp * HEAD_LANES:(mp + 1) * HEAD_LANES]
        return _dot_nt(k_m, q_map(mp, i))

    def score_store(n, u, mp, s, near):
        s_buf[u, mp] = s
        if near:
            s = s_buf[u, mp] + near_ref[0, ta_ref[n]]
            s_buf[u, mp] = s
        mx_buf[u, mp] = jnp.broadcast_to(jnp.max(s, axis=0, keepdims=True), (SUBLANES, t))

    def per_query(x, stat):
        return x.reshape(x.shape[0] // SUBLANES, SUBLANES, t), stat[None]

    def exps(n, u, mp):
        i = ti_ref[n]
        m_old = m_scr[mp, i]
        m_new = jnp.maximum(m_old, mx_buf[u, mp])
        alpha = jnp.exp2(m_old - m_new)
        s3, m3 = per_query(s_buf[u, mp], m_new)
        p = jnp.exp2(s3 - m3).reshape(t, t)
        m_scr[mp, i] = m_new
        p_buf[u, mp] = p.astype(BF16)
        al_buf[u, mp] = alpha

    def pv_dot(n, u, mp):
        j = tj_ref[n]
        vt_m = vt_ref[0, 0, j] if diff else vt_ref[0, 0, j, mp * acc_rows:(mp + 1) * acc_rows, :]
        return _dot(vt_m, p_buf[u, mp]), al_buf[u, mp]

    def acc_update(n, mp, pv, alpha):
        i = ti_ref[n]
        a3, al3 = per_query(acc[mp, i], alpha)
        acc[mp, i] = (al3 * a3).reshape(acc_rows, t) + pv

    items = [(u, mp) for u in range(unroll) for mp in range(2)]

    def step(trip, near_flags, has1=True, has2=True, has3=True):
        pending = {}

        def issue(kk):
            u, mp = items[kk]
            pv = pv_dot(unroll * (trip - 2) + u, u, mp) if has3 else None
            s = score_dot(unroll * trip + u, mp) if has1 else None
            pending[kk] = (s, pv)

        def retire(kk):
            u, mp = items[kk]
            s, pv = pending.pop(kk)
            if has1:
                score_store(unroll * trip + u, u, mp, s, near_flags[u])
            if has3:
                acc_update(unroll * (trip - 2) + u, mp, *pv)

        for kk in range(min(DOT_LEAD, len(items))):
            issue(kk)
        for kk, (u, mp) in enumerate(items):
            if has2:
                exps(unroll * (trip - 1) + u, u, mp)
            if kk + DOT_LEAD < len(items):
                issue(kk + DOT_LEAD)
            if kk + DOT_LEAD - RETIRE_LAG >= 0:
                retire(kk + DOT_LEAD - RETIRE_LAG)
        for kk in sorted(pending):
            retire(kk)

    n_trips = n_tiles // unroll

    def near_flags(trip):
        return [unroll * trip + u < n_near for u in range(unroll)]

    def uniform(trip):
        flags = near_flags(trip)
        return all(flags) or not any(flags)

    step(0, near_flags(0), has2=False, has3=False)
    step(1, near_flags(1), has3=False)
    trip = 2
    while trip < n_trips:
        end = trip + 1
        while uniform(trip) and end < n_trips and near_flags(end) == near_flags(trip):
            end += 1
        if end - trip >= 2:
            flags = near_flags(trip)
            lax.fori_loop(trip, end, lambda tt, carry: (step(tt, flags), carry)[1], 0,
                          unroll=ATTN_LOOP_UNROLL)
        else:
            step(trip, near_flags(trip))
        trip = end
    step(n_trips, None, has1=False)
    step(n_trips + 1, None, has1=False, has2=False)

    if diff:
        lam = (jnp.exp(jnp.sum(lam_ref[0:1] * lam_ref[1:2], axis=-1, keepdims=True))
               - jnp.exp(jnp.sum(lam_ref[2:3] * lam_ref[3:4], axis=-1, keepdims=True)) + lam_init)
        sub_w = subw_ref[...] * (1.0 - lam_init)

    def finalize(i, carry):
        o_a = acc[0, i, :dv] / acc[0, i, dv:dv + 1]
        o_b = acc[1, i, :dv] / acc[1, i, dv:dv + 1]
        if diff:
            o = o_a - lam * o_b
            o = o * lax.rsqrt(jnp.mean(o * o, axis=0, keepdims=True) + RMS_EPS) * sub_w
        else:
            o = jnp.concatenate([o_a, o_b], axis=0)
        o_ref[0, rows(i), :] = o.T.astype(BF16)
        return carry

    lax.fori_loop(0, nq, finalize, 0, unroll=4)


def _attention(q, k, vt, near, lam_vecs=None, subln_w=None, *, diff, lam_init=0.0):
    b, s, _ = q.shape
    t = ATTN_TILE
    nq = s // t
    heads = DA_HEADS
    width = HEAD_LANES if diff else 2 * HEAD_LANES
    acc_rows = (HEAD_LANES if diff else MLA_V) + SUM_ROWS
    assert vt.shape[3] == (1 if diff else 2) * acc_rows
    near_heads = near.shape[0]
    ti, tj, ta, n_near = _tile_tables(nq, near.shape[1])
    n_tiles = len(ti)
    assert n_tiles % ATTN_UNROLL == 0 and n_tiles // ATTN_UNROLL >= 2
    smem = pl.BlockSpec(memory_space=pltpu.SMEM)
    in_specs = [
        smem, smem, smem,
        pl.BlockSpec((1, s, width), lambda bi, h: (bi, 0, h)),
        pl.BlockSpec((1, s, width), lambda bi, h: (bi, 0, h)),
        pl.BlockSpec((1, 1, nq, vt.shape[3], t), lambda bi, h: (bi, h, 0, 0, 0)),
        pl.BlockSpec((1,) + near.shape[1:], lambda bi, h: (h if near_heads > 1 else 0, 0, 0, 0)),
    ]
    args = [jnp.asarray(ti), jnp.asarray(tj), jnp.asarray(ta), q, k, vt, near]
    if diff:
        in_specs += [pl.BlockSpec(lam_vecs.shape, lambda bi, h: (0, 0)),
                     pl.BlockSpec(subln_w.shape, lambda bi, h: (0, 0))]
        args += [lam_vecs, subln_w]
    un = ATTN_UNROLL
    scratch = [pltpu.VMEM((un, 2, t, t), F32), pltpu.VMEM((un, 2, t, t), BF16),
               pltpu.VMEM((un, 2, SUBLANES, t), F32), pltpu.VMEM((un, 2, SUBLANES, t), F32),
               pltpu.VMEM((2, nq, SUBLANES, t), F32), pltpu.VMEM((2, nq, acc_rows, t), F32)]
    if diff:
        scratch.append(pltpu.VMEM((2, s, HEAD_LANES), BF16))
    return pl.pallas_call(
        functools.partial(_attn_kernel, diff=diff, lam_init=lam_init, n_near=n_near, n_tiles=n_tiles),
        grid=(b, heads),
        in_specs=in_specs,
        out_specs=pl.BlockSpec((1, s, HEAD_LANES), lambda bi, h: (bi, 0, h)),
        out_shape=jax.ShapeDtypeStruct((b, s, heads * HEAD_LANES), BF16),
        scratch_shapes=scratch,
        compiler_params=pltpu.CompilerParams(vmem_limit_bytes=VMEM_LIMIT),
        name="attn_diff" if diff else "attn_mla",
    )(*args)


def _post_ffn_kernel(o_ref, x_ref, g1_ref, wo_ref, ln1g_ref, ln1b_ref, sc_ref, sh_ref, g2_ref,
                     w1_ref, w2_ref, ln2g_ref, ln2b_ref, out_ref):
    y = _dot(o_ref[0], wo_ref[...])
    z = DEEPNORM_ALPHA * x_ref[0] + (1.0 + g1_ref[...]) * y
    x1 = _layer_norm_rows(z, ln1g_ref[...], ln1b_ref[...])
    u = (x1 * (1.0 + sc_ref[...]) + sh_ref[...]).astype(BF16)
    y = jnp.zeros(x1.shape, F32)
    for c in range(D_FF // FF_CHUNK):
        sl = slice(c * FF_CHUNK, (c + 1) * FF_CHUNK)
        h = jnp.maximum(_dot(u, w1_ref[:, sl]), 0.0)
        y = y + _dot((h * h).astype(BF16), w2_ref[sl, :])
    z = DEEPNORM_ALPHA * x1 + (1.0 + g2_ref[...]) * y
    out_ref[0] = _layer_norm_rows(z, ln2g_ref[...], ln2b_ref[...])


def _post_ffn(o, x, mod, layer, wo, w1, w2, ln_g, ln_b):
    b, s, d = x.shape
    resident = functools.partial(pl.BlockSpec, index_map=lambda bi, i: (0, 0), pipeline_mode=pl.Buffered(1))
    return pl.pallas_call(
        _post_ffn_kernel,
        grid=(b, s // ROW_TILE),
        in_specs=[_row_spec(d), _row_spec(d), _mod_spec(layer, 2), resident((d, d)),
                  _whole_spec((1, d)), _whole_spec((1, d)), _mod_spec(layer, 4), _mod_spec(layer, 3),
                  _mod_spec(layer, 5), resident((d, D_FF)), resident((D_FF, d)),
                  _whole_spec((1, d)), _whole_spec((1, d))],
        out_specs=_row_spec(d),
        out_shape=jax.ShapeDtypeStruct((b, s, d), F32),
        compiler_params=pltpu.CompilerParams(vmem_limit_bytes=VMEM_LIMIT),
        name=f"post_ffn{layer}",
    )(o, x, mod, wo, ln_g[0][None], ln_b[0][None], mod, mod, mod, w1, w2, ln_g[1][None], ln_b[1][None])


def _rope_tables(pos_offset, s):
    half = MLA_ROPE // 2
    pos = pos_offset[:, None] + jnp.arange(s, dtype=jnp.int32)[None, :]
    inv = ROPE_THETA ** (-jnp.arange(half, dtype=F32) / half)
    ang = pos.astype(F32)[..., None] * inv
    cos = jnp.cos(ang)
    sin = jnp.sin(ang)
    b = pos.shape[0]
    ones = jnp.ones((b, s, MLA_NOPE), F32)
    zeros_tail = jnp.zeros((b, s, HEAD_LANES - MLA_NOPE - MLA_ROPE), F32)
    cos_t = jnp.concatenate([ones, cos, cos, zeros_tail], axis=-1)
    sin_t = jnp.concatenate([jnp.zeros_like(ones), -sin, sin, zeros_tail], axis=-1)
    return cos_t, sin_t


def _pad_heads(w, per_head, lead):
    w = w.reshape(lead, MLA_HEADS, per_head)
    w = jnp.pad(w, ((0, 0), (0, 0), (0, HEAD_LANES - per_head)))
    return w.reshape(lead, MLA_HEADS * HEAD_LANES)


def kernel(x, c, pos_offset, ada_w, ada_b, ln_g, ln_b, rel_table, da_w_qkv, da_w_o, da_lam_q1, da_lam_k1, da_lam_q2, da_lam_k2, da_subln_w, mla_w_down, mla_q_norm_w, mla_w_uq, mla_kv_norm_w, mla_w_ukv, mla_w_o, ffn_w1, ffn_w2):
    b, s, d = x.shape
    assert d == D_MODEL and s % ROW_TILE == 0 and ROW_TILE % ATTN_TILE == 0 and ATTN_TILE % CHUNK == 0

    mod = _modulation(c, ada_w, ada_b)

    w_qkv = da_w_qkv[0].astype(BF16)
    q, k, vt = _pre0(x, mod, w_qkv[:, :d], w_qkv[:, d:2 * d], w_qkv[:, 2 * d:].T)
    lam_vecs = jnp.stack([da_lam_q1[0], da_lam_k1[0], da_lam_q2[0], da_lam_k2[0]])
    lam_init = 0.8 - 0.6 * math.exp(-0.3 * 0)
    o = _attention(q, k, vt, _near_bias(rel_table), lam_vecs, da_subln_w[0].reshape(-1, 1),
                   diff=True, lam_init=lam_init)
    x = _post_ffn(o, x, mod, 0, da_w_o[0].astype(BF16), ffn_w1[0].astype(BF16), ffn_w2[0].astype(BF16),
                  ln_g[0], ln_b[0])

    n_lat = MLA_Q_LORA + MLA_KV_LORA
    w_down = mla_w_down[0]
    w_down = jnp.concatenate(
        [w_down[:, :n_lat], jnp.zeros((d, MLA_NOPE), F32), w_down[:, n_lat:],
         jnp.zeros((d, HEAD_LANES - MLA_NOPE - MLA_ROPE), F32)], axis=-1).astype(BF16)
    w_uq = _pad_heads(mla_w_uq[0], MLA_NOPE + MLA_ROPE, MLA_Q_LORA).astype(BF16)
    w_ukv = mla_w_ukv[0].reshape(MLA_KV_LORA, MLA_HEADS, MLA_NOPE + MLA_V)
    w_uk = _pad_heads(w_ukv[..., :MLA_NOPE].reshape(MLA_KV_LORA, -1), MLA_NOPE, MLA_KV_LORA).astype(BF16)
    w_uvt = w_ukv[..., MLA_NOPE:].reshape(MLA_KV_LORA, -1).T.astype(BF16)
    cos_t, sin_t = _rope_tables(pos_offset, s)
    q, k, vt = _pre1(x, mod, w_down, mla_q_norm_w[0][None], mla_kv_norm_w[0][None],
                     w_uq, w_uk, w_uvt, cos_t, sin_t)
    o = _attention(q, k, vt, _near_mask(), diff=False)
    return _post_ffn(o, x, mod, 1, mla_w_o[0].astype(BF16), ffn_w1[1].astype(BF16), ffn_w2[1].astype(BF16),
                     ln_g[1], ln_b[1])
```

```python
import functools
import math

import jax
import jax.numpy as jnp
import numpy as np
from jax import lax
from jax.experimental import pallas as pl
from jax.experimental.pallas import tpu as pltpu

F32 = jnp.float32
BF16 = jnp.bfloat16

D_MODEL = 1024
DEPTH = 2
CHUNK = 64
DA_HEAD_DIM = 64
DA_HEADS = D_MODEL // (2 * DA_HEAD_DIM)
MLA_HEADS = 16
MLA_NOPE = 64
MLA_ROPE = 32
MLA_V = 64
MLA_Q_LORA = 512
MLA_KV_LORA = 256
ROPE_THETA = 10000.0
REL_BUCKETS = 32
REL_MAX_DIST = 128
D_FF = 4 * D_MODEL
DEEPNORM_ALPHA = (2 * DEPTH) ** 0.25
LN_EPS = 1e-5
RMS_EPS = 1e-6

LANES = 128
SUBLANES = 8
HEAD_LANES = 128
ATTN_TILE = 256
ATTN_UNROLL = 8
ATTN_LOOP_UNROLL = 0
DOT_LEAD = 2
RETIRE_LAG = 3
SUM_ROWS = 16
ROW_TILE = 1024
FF_CHUNK = 1024
LOG2E = math.log2(math.e)
NEG_BIG = -1e30
VMEM_LIMIT = 56 * 1024 * 1024

_NT = (((1,), (1,)), ((), ()))


def _dot(a, b):
    return jnp.dot(a, b, preferred_element_type=F32)


def _dot_nt(a, b):
    return lax.dot_general(a, b, _NT, preferred_element_type=F32)


def _layer_norm_rows(z, g, b):
    mu = jnp.mean(z, axis=-1, keepdims=True)
    zc = z - mu
    var = jnp.mean(zc * zc, axis=-1, keepdims=True)
    return zc * lax.rsqrt(var + LN_EPS) * g + b


def _rms_norm_rows(z, w):
    return z * lax.rsqrt(jnp.mean(z * z, axis=-1, keepdims=True) + RMS_EPS) * w


def _mod_kernel(c_ref, w_ref, b_ref, o_ref):
    c = c_ref[...]
    c_act = c * jax.nn.sigmoid(c)
    o_ref[0, 0] = _dot(c_act.astype(BF16), w_ref[0].astype(BF16)) + b_ref[0, 0]


def _modulation(c, ada_w, ada_b):
    b = c.shape[0]
    rows = 8
    c_pad = jnp.zeros((rows, D_MODEL), F32).at[:b].set(c)
    out = pl.pallas_call(
        _mod_kernel,
        grid=(DEPTH, 6),
        in_specs=[
            pl.BlockSpec((rows, D_MODEL), lambda i, j: (0, 0)),
            pl.BlockSpec((1, D_MODEL, D_MODEL), lambda i, j: (i, 0, j)),
            pl.BlockSpec((1, 1, 1, D_MODEL), lambda i, j: (i, j, 0, 0)),
        ],
        out_specs=pl.BlockSpec((1, 1, rows, D_MODEL), lambda i, j: (i, j, 0, 0)),
        out_shape=jax.ShapeDtypeStruct((DEPTH, 6, rows, D_MODEL), F32),
        compiler_params=pltpu.CompilerParams(vmem_limit_bytes=VMEM_LIMIT),
        name="mod",
    )(c_pad, ada_w, ada_b.reshape(DEPTH, 6, 1, D_MODEL))
    return out[:, :, :b].reshape(DEPTH, 6, b, 1, D_MODEL)


def _mod_spec(layer, idx):
    return pl.BlockSpec((None, None, None, 1, D_MODEL), lambda b, i: (layer, idx, b, 0, 0))


def _row_spec(width):
    return pl.BlockSpec((1, ROW_TILE, width), lambda b, i: (b, i, 0))


def _whole_spec(shape):
    return pl.BlockSpec(shape, lambda b, i: (0,) * len(shape))


def _vt_rows(maps_per_head):
    return HEAD_LANES + maps_per_head * SUM_ROWS


def _vt_shape(b, s, maps_per_head):
    return (b, DA_HEADS, s // ATTN_TILE, _vt_rows(maps_per_head), ATTN_TILE)


def _vt_out_spec(maps_per_head):
    return pl.BlockSpec((1, DA_HEADS, ROW_TILE // ATTN_TILE, _vt_rows(maps_per_head), ATTN_TILE),
                        lambda b, i: (b, 0, i, 0, 0))


def _store_vt(vt_ref, vt, maps_per_head):
    groups = DA_HEADS * maps_per_head
    vt = vt.astype(BF16).reshape(groups, HEAD_LANES // maps_per_head, ROW_TILE)
    ones = jnp.ones((groups, SUM_ROWS, ROW_TILE), BF16)
    vt = jnp.concatenate([vt, ones], axis=1).reshape(DA_HEADS, _vt_rows(maps_per_head), ROW_TILE)
    for t in range(ROW_TILE // ATTN_TILE):
        vt_ref[0, :, t] = vt[:, :, t * ATTN_TILE:(t + 1) * ATTN_TILE]


def _pre0_kernel(x_ref, sc_ref, sh_ref, wq_ref, wk_ref, wvt_ref, q_ref, k_ref, vt_ref, *, q_scale):
    u = (x_ref[0] * (1.0 + sc_ref[...]) + sh_ref[...]).astype(BF16)
    q_ref[0] = (_dot(u, wq_ref[...]) * q_scale).astype(BF16)
    k_ref[0] = _dot(u, wk_ref[...]).astype(BF16)
    _store_vt(vt_ref, _dot_nt(wvt_ref[...], u), 1)


def _pre0(x, mod, wq, wk, wvt):
    b, s, d = x.shape
    q_scale = DA_HEAD_DIM ** -0.5 * LOG2E
    return pl.pallas_call(
        functools.partial(_pre0_kernel, q_scale=q_scale),
        grid=(b, s // ROW_TILE),
        in_specs=[_row_spec(d), _mod_spec(0, 1), _mod_spec(0, 0),
                  _whole_spec((d, d)), _whole_spec((d, d)), _whole_spec((d, d))],
        out_specs=[_row_spec(d), _row_spec(d), _vt_out_spec(1)],
        out_shape=[jax.ShapeDtypeStruct((b, s, d), BF16),
                   jax.ShapeDtypeStruct((b, s, d), BF16),
                   jax.ShapeDtypeStruct(_vt_shape(b, s, 1), BF16)],
        compiler_params=pltpu.CompilerParams(vmem_limit_bytes=VMEM_LIMIT),
        name="pre0",
    )(x, mod, mod, wq, wk, wvt)


def _pre1_kernel(x_ref, sc_ref, sh_ref, wd_ref, qn_ref, kvn_ref, wuq_ref, wuk_ref, wuvt_ref,
                 cos_ref, sin_ref, q_ref, k_ref, vt_ref, *, q_scale):
    u = (x_ref[0] * (1.0 + sc_ref[...]) + sh_ref[...]).astype(BF16)
    down = _dot(u, wd_ref[...])
    c_q = _rms_norm_rows(down[:, :MLA_Q_LORA], qn_ref[...]).astype(BF16)
    c_kv = _rms_norm_rows(down[:, MLA_Q_LORA:MLA_Q_LORA + MLA_KV_LORA], kvn_ref[...]).astype(BF16)
    k_rope = down[:, MLA_Q_LORA + MLA_KV_LORA:]
    q = _dot(c_q, wuq_ref[...])
    k_nope = _dot(c_kv, wuk_ref[...])
    _store_vt(vt_ref, _dot_nt(wuvt_ref[...], c_kv), 2)

    cos = cos_ref[0]
    sin = sin_ref[0]
    lane = lax.broadcasted_iota(jnp.int32, cos.shape, 1)
    first_half = lane < MLA_NOPE + MLA_ROPE // 2

    def rope(xh):
        partner = jnp.where(first_half,
                            pltpu.roll(xh, HEAD_LANES - MLA_ROPE // 2, 1),
                            pltpu.roll(xh, MLA_ROPE // 2, 1))
        return xh * cos + partner * sin

    k_rope = rope(k_rope)
    for h in range(MLA_HEADS):
        sl = slice(h * HEAD_LANES, (h + 1) * HEAD_LANES)
        q_ref[0, :, sl] = (rope(q[:, sl]) * q_scale).astype(BF16)
        k_ref[0, :, sl] = (k_nope[:, sl] + k_rope).astype(BF16)


def _pre1(x, mod, wd, qn, kvn, wuq, wuk, wuvt, cos_t, sin_t):
    b, s, d = x.shape
    hw = MLA_HEADS * HEAD_LANES
    q_scale = (MLA_NOPE + MLA_ROPE) ** -0.5 * LOG2E
    return pl.pallas_call(
        functools.partial(_pre1_kernel, q_scale=q_scale),
        grid=(b, s // ROW_TILE),
        in_specs=[_row_spec(d), _mod_spec(1, 1), _mod_spec(1, 0),
                  _whole_spec(wd.shape), _whole_spec(qn.shape), _whole_spec(kvn.shape),
                  _whole_spec(wuq.shape), _whole_spec(wuk.shape), _whole_spec(wuvt.shape),
                  _row_spec(HEAD_LANES), _row_spec(HEAD_LANES)],
        out_specs=[_row_spec(hw), _row_spec(hw), _vt_out_spec(2)],
        out_shape=[jax.ShapeDtypeStruct((b, s, hw), BF16),
                   jax.ShapeDtypeStruct((b, s, hw), BF16),
                   jax.ShapeDtypeStruct(_vt_shape(b, s, 2), BF16)],
        compiler_params=pltpu.CompilerParams(vmem_limit_bytes=VMEM_LIMIT),
        name="pre1",
    )(x, mod, mod, wd, qn, kvn, wuq, wuk, wuvt, cos_t, sin_t)


def _t5_bucket(rel):
    nb = REL_BUCKETS // 2
    max_exact = nb // 2
    n_log = nb - max_exact
    thresholds = []
    for k in range(1, n_log):
        n = max_exact
        while n ** n_log * max_exact ** k < REL_MAX_DIST ** k * max_exact ** n_log:
            n += 1
        thresholds.append(n)
    n = np.abs(rel)
    large = max_exact + sum((n >= thr).astype(np.int32) for thr in thresholds)
    return (rel > 0).astype(np.int32) * nb + np.where(n < max_exact, n, large)


def _near_bucket_tiles():
    kk = np.arange(ATTN_TILE, dtype=np.int32)[:, None]
    qq = np.arange(ATTN_TILE, dtype=np.int32)[None, :]
    diag = np.where(kk // CHUNK <= qq // CHUNK, _t5_bucket(kk - qq), -1)
    prev = _t5_bucket(kk - ATTN_TILE - qq)
    return jnp.asarray(np.stack([diag, prev]).astype(np.int32))


def _bias_kernel(tab_ref, idx_ref, o_ref):
    h = pl.program_id(0)
    idx = idx_ref[...]
    val = jnp.zeros(idx.shape, F32)
    for bkt in range(REL_BUCKETS):
        val = jnp.where(idx == bkt, tab_ref[bkt, h], val)
    far = tab_ref[REL_BUCKETS // 2 - 1, h]
    o_ref[0] = jnp.where(idx < 0, NEG_BIG, (val - far) * LOG2E)


def _near_bias(rel_table):
    idx = _near_bucket_tiles()
    return pl.pallas_call(
        _bias_kernel,
        grid=(DA_HEADS,),
        in_specs=[pl.BlockSpec(memory_space=pltpu.SMEM),
                  pl.BlockSpec(idx.shape, lambda h: (0, 0, 0))],
        out_specs=pl.BlockSpec((1,) + idx.shape, lambda h: (h, 0, 0, 0)),
        out_shape=jax.ShapeDtypeStruct((DA_HEADS,) + idx.shape, F32),
        name="near_bias",
    )(rel_table, idx)


def _near_mask():
    kk = jnp.arange(ATTN_TILE, dtype=jnp.int32)[:, None]
    qq = jnp.arange(ATTN_TILE, dtype=jnp.int32)[None, :]
    return jnp.where(kk // CHUNK <= qq // CHUNK, 0.0, NEG_BIG).astype(F32)[None, None]


def _tile_tables(nq, n_near_kinds):
    ti, tj, ta = [], [], []
    for kind in range(n_near_kinds):
        for i in range(kind, nq):
            ti.append(i), tj.append(i - kind), ta.append(kind)
    n_near = len(ti)
    for j in range(nq - n_near_kinds):
        for i in range(j + n_near_kinds, nq):
            ti.append(i), tj.append(j), ta.append(0)
    return np.asarray(ti, np.int32), np.asarray(tj, np.int32), np.asarray(ta, np.int32), n_near


def _attn_kernel(*refs, diff, lam_init, n_near, n_tiles, tables):
    ti_ref, tj_ref, ta_ref, q_ref, k_ref, vt_ref, near_ref = refs[:7]
    refs = refs[7:]
    if diff:
        lam_ref, subw_ref = refs[:2]
        refs = refs[2:]
    o_ref, s_buf, p_buf, mx_buf, al_buf, m_scr, acc = refs[:7]
    t = ATTN_TILE
    unroll = s_buf.shape[0]
    nq = acc.shape[1]
    acc_rows = acc.shape[2]
    dv = acc_rows - SUM_ROWS

    if diff:
        qm = refs[7]
        q_all = q_ref[0]
        lane = lax.broadcasted_iota(jnp.int32, q_all.shape, 1)
        zero = jnp.zeros_like(q_all)
        qm[0] = jnp.where(lane < DA_HEAD_DIM, q_all, zero)
        qm[1] = jnp.where(lane >= DA_HEAD_DIM, q_all, zero)

    m_scr[...] = jnp.full(m_scr.shape, NEG_BIG, F32)
    acc[...] = jnp.zeros(acc.shape, F32)

    def rows(idx):
        if isinstance(idx, int):
            return pl.ds(idx * t, t)
        return pl.ds(pl.multiple_of(idx * t, t), t)

    def lookup(table, ref, n):
        return int(table[n]) if isinstance(n, int) else ref[n]

    ti_tab, tj_tab, ta_tab = tables

    def q_map(mp, i):
        if diff:
            return qm[mp, rows(i), :]
        return q_ref[0, rows(i), mp * HEAD_LANES:(mp + 1) * HEAD_LANES]

    def score_dot(n, mp):
        i, j = lookup(ti_tab, ti_ref, n), lookup(tj_tab, tj_ref, n)
        k_m =k_ref[0, rows(j), :] if diff else k_ref[0, rows(j), mp * HEAD_LANES:(mp + 1) * HEAD_LANES]
        return _dot_nt(k_m, q_map(mp, i))

    def score_store(n, u, mp, s, near):
        s_buf[u, mp] = s
        if near:
            s = s_buf[u, mp] + near_ref[0, lookup(ta_tab, ta_ref, n)]
            s_buf[u, mp] = s
        mx_buf[u, mp] = jnp.broadcast_to(jnp.max(s, axis=0, keepdims=True), (SUBLANES, t))

    def per_query(x, stat):
        return x.reshape(x.shape[0] // SUBLANES, SUBLANES, t), stat[None]

    def exps(n, u, mp):
        i = lookup(ti_tab, ti_ref, n)
        m_old = m_scr[mp, i]
        m_new = jnp.maximum(m_old, mx_buf[u, mp])
        alpha = jnp.exp2(m_old - m_new)
        s3, m3 = per_query(s_buf[u, mp], m_new)
        p = jnp.exp2(s3 - m3).reshape(t, t)
        m_scr[mp, i] = m_new
        p_buf[u, mp] = p.astype(BF16)
        al_buf[u, mp] = alpha

    def pv_dot(n, u, mp):
        j = lookup(tj_tab, tj_ref, n)
        vt_m =vt_ref[0, 0, j] if diff else vt_ref[0, 0, j, mp * acc_rows:(mp + 1) * acc_rows, :]
        return _dot(vt_m, p_buf[u, mp]), al_buf[u, mp]

    def acc_update(n, mp, pv, alpha):
        i = lookup(ti_tab, ti_ref, n)
        a3, al3 = per_query(acc[mp, i], alpha)
        acc[mp, i] = (al3 * a3).reshape(acc_rows, t) + pv

    items = [(u, mp) for u in range(unroll) for mp in range(2)]

    def step(trip, near_flags, has1=True, has2=True, has3=True):
        pending = {}

        def issue(kk):
            u, mp = items[kk]
            pv = pv_dot(unroll * (trip - 2) + u, u, mp) if has3 else None
            s = score_dot(unroll * trip + u, mp) if has1 else None
            pending[kk] = (s, pv)

        def retire(kk):
            u, mp = items[kk]
            s, pv = pending.pop(kk)
            if has1:
                score_store(unroll * trip + u, u, mp, s, near_flags[u])
            if has3:
                acc_update(unroll * (trip - 2) + u, mp, *pv)

        for kk in range(min(DOT_LEAD, len(items))):
            issue(kk)
        for kk, (u, mp) in enumerate(items):
            if has2:
                exps(unroll * (trip - 1) + u, u, mp)
            if kk + DOT_LEAD < len(items):
                issue(kk + DOT_LEAD)
            if kk + DOT_LEAD - RETIRE_LAG >= 0:
                retire(kk + DOT_LEAD - RETIRE_LAG)
        for kk in sorted(pending):
            retire(kk)

    n_trips = n_tiles // unroll

    def near_flags(trip):
        return [unroll * trip + u < n_near for u in range(unroll)]

    def uniform(trip):
        flags = near_flags(trip)
        return all(flags) or not any(flags)

    step(0, near_flags(0), has2=False, has3=False)
    step(1, near_flags(1), has3=False)
    trip = 2
    while trip < n_trips:
        end = trip + 1
        while uniform(trip) and end < n_trips and near_flags(end) == near_flags(trip):
            end += 1
        if end - trip >= 2 and ATTN_LOOP_UNROLL > 0:
            flags = near_flags(trip)
            lax.fori_loop(trip, end, lambda tt, carry: (step(tt, flags), carry)[1], 0,
                          unroll=ATTN_LOOP_UNROLL)
        else:
            for tt in range(trip, end):
                step(tt, near_flags(tt))
        trip = end
    step(n_trips, None, has1=False)
    step(n_trips + 1, None, has1=False, has2=False)

    if diff:
        lam = (jnp.exp(jnp.sum(lam_ref[0:1] * lam_ref[1:2], axis=-1, keepdims=True))
               - jnp.exp(jnp.sum(lam_ref[2:3] * lam_ref[3:4], axis=-1, keepdims=True)) + lam_init)
        sub_w = subw_ref[...] * (1.0 - lam_init)

    def finalize(i, carry):
        o_a = acc[0, i, :dv] / acc[0, i, dv:dv + 1]
        o_b = acc[1, i, :dv] / acc[1, i, dv:dv + 1]
        if diff:
            o = o_a - lam * o_b
            o = o * lax.rsqrt(jnp.mean(o * o, axis=0, keepdims=True) + RMS_EPS) * sub_w
        else:
            o = jnp.concatenate([o_a, o_b], axis=0)
        o_ref[0, rows(i), :] = o.T.astype(BF16)
        return carry

    lax.fori_loop(0, nq, finalize, 0, unroll=4)


def _attention(q, k, vt, near, lam_vecs=None, subln_w=None, *, diff, lam_init=0.0):
    b, s, _ = q.shape
    t = ATTN_TILE
    nq = s // t
    heads = DA_HEADS
    width = HEAD_LANES if diff else 2 * HEAD_LANES
    acc_rows = (HEAD_LANES if diff else MLA_V) + SUM_ROWS
    assert vt.shape[3] == (1 if diff else 2) * acc_rows
    near_heads = near.shape[0]
    ti, tj, ta, n_near = _tile_tables(nq, near.shape[1])
    n_tiles = len(ti)
    assert n_tiles % ATTN_UNROLL == 0 and n_tiles // ATTN_UNROLL >= 2
    smem = pl.BlockSpec(memory_space=pltpu.SMEM)
    in_specs = [
        smem, smem, smem,
        pl.BlockSpec((1, s, width), lambda bi, h: (bi, 0, h)),
        pl.BlockSpec((1, s, width), lambda bi, h: (bi, 0, h)),
        pl.BlockSpec((1, 1, nq, vt.shape[3], t), lambda bi, h: (bi, h, 0, 0, 0)),
        pl.BlockSpec((1,) + near.shape[1:], lambda bi, h: (h if near_heads > 1 else 0, 0, 0, 0)),
    ]
    args = [jnp.asarray(ti), jnp.asarray(tj), jnp.asarray(ta), q, k, vt, near]
    if diff:
        in_specs += [pl.BlockSpec(lam_vecs.shape, lambda bi, h: (0, 0)),
                     pl.BlockSpec(subln_w.shape, lambda bi, h: (0, 0))]
        args += [lam_vecs, subln_w]
    un = ATTN_UNROLL
    scratch = [pltpu.VMEM((un, 2, t, t), F32), pltpu.VMEM((un, 2, t, t), BF16),
               pltpu.VMEM((un, 2, SUBLANES, t), F32), pltpu.VMEM((un, 2, SUBLANES, t), F32),
               pltpu.VMEM((2, nq, SUBLANES, t), F32), pltpu.VMEM((2, nq, acc_rows, t), F32)]
    if diff:
        scratch.append(pltpu.VMEM((2, s, HEAD_LANES), BF16))
    return pl.pallas_call(
        functools.partial(_attn_kernel, diff=diff, lam_init=lam_init, n_near=n_near, n_tiles=n_tiles,
                          tables=(ti, tj, ta)),
        grid=(b, heads),
        in_specs=in_specs,
        out_specs=pl.BlockSpec((1, s, HEAD_LANES), lambda bi, h: (bi, 0, h)),
        out_shape=jax.ShapeDtypeStruct((b, s, heads * HEAD_LANES), BF16),
        scratch_shapes=scratch,
        compiler_params=pltpu.CompilerParams(vmem_limit_bytes=VMEM_LIMIT),
        name="attn_diff" if diff else "attn_mla",
    )(*args)


def _post_ffn_kernel(o_ref, x_ref, g1_ref, wo_ref, ln1g_ref, ln1b_ref, sc_ref, sh_ref, g2_ref,
                     w1_ref, w2_ref, ln2g_ref, ln2b_ref, out_ref):
    y = _dot(o_ref[0], wo_ref[...])
    z = DEEPNORM_ALPHA * x_ref[0] + (1.0 + g1_ref[...]) * y
    x1 = _layer_norm_rows(z, ln1g_ref[...], ln1b_ref[...])
    u = (x1 * (1.0 + sc_ref[...]) + sh_ref[...]).astype(BF16)
    y = jnp.zeros(x1.shape, F32)
    for c in range(D_FF // FF_CHUNK):
        sl = slice(c * FF_CHUNK, (c + 1) * FF_CHUNK)
        h = jnp.maximum(_dot(u, w1_ref[:, sl]), 0.0)
        y = y + _dot((h * h).astype(BF16), w2_ref[sl, :])
    z = DEEPNORM_ALPHA * x1 + (1.0 + g2_ref[...]) * y
    out_ref[0] = _layer_norm_rows(z, ln2g_ref[...], ln2b_ref[...])


def _post_ffn(o, x, mod, layer, wo, w1, w2, ln_g, ln_b):
    b, s, d = x.shape
    resident = functools.partial(pl.BlockSpec, index_map=lambda bi, i: (0, 0), pipeline_mode=pl.Buffered(1))
    return pl.pallas_call(
        _post_ffn_kernel,
        grid=(b, s // ROW_TILE),
        in_specs=[_row_spec(d), _row_spec(d), _mod_spec(layer, 2), resident((d, d)),
                  _whole_spec((1, d)), _whole_spec((1, d)), _mod_spec(layer, 4), _mod_spec(layer, 3),
                  _mod_spec(layer, 5), resident((d, D_FF)), resident((D_FF, d)),
                  _whole_spec((1, d)), _whole_spec((1, d))],
        out_specs=_row_spec(d),
        out_shape=jax.ShapeDtypeStruct((b, s, d), F32),
        compiler_params=pltpu.CompilerParams(vmem_limit_bytes=VMEM_LIMIT),
        name=f"post_ffn{layer}",
    )(o, x, mod, wo, ln_g[0][None], ln_b[0][None], mod, mod, mod, w1, w2, ln_g[1][None], ln_b[1][None])


def _rope_tables(pos_offset, s):
    half = MLA_ROPE // 2
    pos = pos_offset[:, None] + jnp.arange(s, dtype=jnp.int32)[None, :]
    inv = ROPE_THETA ** (-jnp.arange(half, dtype=F32) / half)
    ang = pos.astype(F32)[..., None] * inv
    cos = jnp.cos(ang)
    sin = jnp.sin(ang)
    b = pos.shape[0]
    ones = jnp.ones((b, s, MLA_NOPE), F32)
    zeros_tail = jnp.zeros((b, s, HEAD_LANES - MLA_NOPE - MLA_ROPE), F32)
    cos_t = jnp.concatenate([ones, cos, cos, zeros_tail], axis=-1)
    sin_t = jnp.concatenate([jnp.zeros_like(ones), -sin, sin, zeros_tail], axis=-1)
    return cos_t, sin_t


def _pad_heads(w, per_head, lead):
    w = w.reshape(lead, MLA_HEADS, per_head)
    w = jnp.pad(w, ((0, 0), (0, 0), (0, HEAD_LANES - per_head)))
    return w.reshape(lead, MLA_HEADS * HEAD_LANES)


def kernel(x, c, pos_offset, ada_w, ada_b, ln_g, ln_b, rel_table, da_w_qkv, da_w_o, da_lam_q1, da_lam_k1, da_lam_q2, da_lam_k2, da_subln_w, mla_w_down, mla_q_norm_w, mla_w_uq, mla_kv_norm_w, mla_w_ukv, mla_w_o, ffn_w1, ffn_w2):
    b, s, d = x.shape
    assert d == D_MODEL and s % ROW_TILE == 0 and ROW_TILE % ATTN_TILE == 0 and ATTN_TILE % CHUNK == 0

    mod = _modulation(c, ada_w, ada_b)

    w_qkv = da_w_qkv[0].astype(BF16)
    q, k, vt = _pre0(x, mod, w_qkv[:, :d], w_qkv[:, d:2 * d], w_qkv[:, 2 * d:].T)
    lam_vecs = jnp.stack([da_lam_q1[0], da_lam_k1[0], da_lam_q2[0], da_lam_k2[0]])
    lam_init = 0.8 - 0.6 * math.exp(-0.3 * 0)
    o = _attention(q, k, vt, _near_bias(rel_table), lam_vecs, da_subln_w[0].reshape(-1, 1),
                   diff=True, lam_init=lam_init)
    x = _post_ffn(o, x, mod, 0, da_w_o[0].astype(BF16), ffn_w1[0].astype(BF16), ffn_w2[0].astype(BF16),
                  ln_g[0], ln_b[0])

    n_lat = MLA_Q_LORA + MLA_KV_LORA
    w_down = mla_w_down[0]
    w_down = jnp.concatenate(
        [w_down[:, :n_lat], jnp.zeros((d, MLA_NOPE), F32), w_down[:, n_lat:],
         jnp.zeros((d, HEAD_LANES - MLA_NOPE - MLA_ROPE), F32)], axis=-1).astype(BF16)
    w_uq = _pad_heads(mla_w_uq[0], MLA_NOPE + MLA_ROPE, MLA_Q_LORA).astype(BF16)
    w_ukv = mla_w_ukv[0].reshape(MLA_KV_LORA, MLA_HEADS, MLA_NOPE + MLA_V)
    w_uk = _pad_heads(w_ukv[..., :MLA_NOPE].reshape(MLA_KV_LORA, -1), MLA_NOPE, MLA_KV_LORA).astype(BF16)
    w_uvt = w_ukv[..., MLA_NOPE:].reshape(MLA_KV_LORA, -1).T.astype(BF16)
    cos_t, sin_t = _rope_tables(pos_offset, s)
    q, k, vt = _pre1(x, mod, w_down, mla_q_norm_w[0][None], mla_kv_norm_w[0][None],
                     w_uq, w_uk, w_uvt, cos_t, sin_t)
    o = _attention(q, k, vt, _near_mask(), diff=False)
    return _post_ffn(o, x, mod, 1, mla_w_o[0].astype(BF16), ffn_w1[1].astype(BF16), ffn_w2[1].astype(BF16),
                     ln_g[1], ln_b[1])
```

```python
import functools
import math

import jax
import jax.numpy as jnp
import numpy as np
from jax import lax
from jax.experimental import pallas as pl
from jax.experimental.pallas import tpu as pltpu

F32 = jnp.float32
BF16 = jnp.bfloat16

D_MODEL = 1024
DEPTH = 2
CHUNK = 64
DA_HEAD_DIM = 64
DA_HEADS = D_MODEL // (2 * DA_HEAD_DIM)
MLA_HEADS = 16
MLA_NOPE = 64
MLA_ROPE = 32
MLA_V = 64
MLA_Q_LORA = 512
MLA_KV_LORA = 256
ROPE_THETA = 10000.0
REL_BUCKETS = 32
REL_MAX_DIST = 128
D_FF = 4 * D_MODEL
DEEPNORM_ALPHA = (2 * DEPTH) ** 0.25
LN_EPS = 1e-5
RMS_EPS = 1e-6

LANES = 128
SUBLANES = 8
HEAD_LANES = 128
ATTN_TILE = 256
ATTN_UNROLL = 2
ATTN_LOOP_UNROLL = 0
DOT_LEAD = 2
RETIRE_LAG = 3
SUM_ROWS = 16
ROW_TILE = 1024
FF_CHUNK = 1024
LOG2E = math.log2(math.e)
NEG_BIG = -1e30
VMEM_LIMIT = 56 * 1024 * 1024

_NT = (((1,), (1,)), ((), ()))


def _dot(a, b):
    return jnp.dot(a, b, preferred_element_type=F32)


def _dot_nt(a, b):
    return lax.dot_general(a, b, _NT, preferred_element_type=F32)


def _layer_norm_rows(z, g, b):
    mu = jnp.mean(z, axis=-1, keepdims=True)
    zc = z - mu
    var = jnp.mean(zc * zc, axis=-1, keepdims=True)
    return zc * lax.rsqrt(var + LN_EPS) * g + b


def _rms_norm_rows(z, w):
    return z * lax.rsqrt(jnp.mean(z * z, axis=-1, keepdims=True) + RMS_EPS) * w


def _mod_kernel(c_ref, w_ref, b_ref, o_ref):
    c = c_ref[...]
    c_act = c * jax.nn.sigmoid(c)
    o_ref[0, 0] = _dot(c_act.astype(BF16), w_ref[0].astype(BF16)) + b_ref[0, 0]


def _modulation(c, ada_w, ada_b):
    b = c.shape[0]
    rows = 8
    c_pad = jnp.zeros((rows, D_MODEL), F32).at[:b].set(c)
    out = pl.pallas_call(
        _mod_kernel,
        grid=(DEPTH, 6),
        in_specs=[
            pl.BlockSpec((rows, D_MODEL), lambda i, j: (0, 0)),
            pl.BlockSpec((1, D_MODEL, D_MODEL), lambda i, j: (i, 0, j)),
            pl.BlockSpec((1, 1, 1, D_MODEL), lambda i, j: (i, j, 0, 0)),
        ],
        out_specs=pl.BlockSpec((1, 1, rows, D_MODEL), lambda i, j: (i, j, 0, 0)),
        out_shape=jax.ShapeDtypeStruct((DEPTH, 6, rows, D_MODEL), F32),
        compiler_params=pltpu.CompilerParams(vmem_limit_bytes=VMEM_LIMIT),
        name="mod",
    )(c_pad, ada_w, ada_b.reshape(DEPTH, 6, 1, D_MODEL))
    return out[:, :, :b].reshape(DEPTH, 6, b, 1, D_MODEL)


def _mod_spec(layer, idx):
    return pl.BlockSpec((None, None, None, 1, D_MODEL), lambda b, i: (layer, idx, b, 0, 0))


def _row_spec(width):
    return pl.BlockSpec((1, ROW_TILE, width), lambda b, i: (b, i, 0))


def _whole_spec(shape):
    return pl.BlockSpec(shape, lambda b, i: (0,) * len(shape))


def _vt_rows(maps_per_head):
    return HEAD_LANES + maps_per_head * SUM_ROWS


def _vt_shape(b, s, maps_per_head):
    return (b, DA_HEADS, s // ATTN_TILE, _vt_rows(maps_per_head), ATTN_TILE)


def _vt_out_spec(maps_per_head):
    return pl.BlockSpec((1, DA_HEADS, ROW_TILE // ATTN_TILE, _vt_rows(maps_per_head), ATTN_TILE),
                        lambda b, i: (b, 0, i, 0, 0))


def _store_vt(vt_ref, vt, maps_per_head):
    groups = DA_HEADS * maps_per_head
    vt = vt.astype(BF16).reshape(groups, HEAD_LANES // maps_per_head, ROW_TILE)
    ones = jnp.ones((groups, SUM_ROWS, ROW_TILE), BF16)
    vt = jnp.concatenate([vt, ones], axis=1).reshape(DA_HEADS, _vt_rows(maps_per_head), ROW_TILE)
    for t in range(ROW_TILE // ATTN_TILE):
        vt_ref[0, :, t] = vt[:, :, t * ATTN_TILE:(t + 1) * ATTN_TILE]


def _pre0_kernel(x_ref, sc_ref, sh_ref, wq_ref, wk_ref, wvt_ref, q_ref, k_ref, vt_ref, *, q_scale):
    u = (x_ref[0] * (1.0 + sc_ref[...]) + sh_ref[...]).astype(BF16)
    q_ref[0] = (_dot(u, wq_ref[...]) * q_scale).astype(BF16)
    k_ref[0] = _dot(u, wk_ref[...]).astype(BF16)
    _store_vt(vt_ref, _dot_nt(wvt_ref[...], u), 1)


def _pre0(x, mod, wq, wk, wvt):
    b, s, d = x.shape
    q_scale = DA_HEAD_DIM ** -0.5 * LOG2E
    return pl.pallas_call(
        functools.partial(_pre0_kernel, q_scale=q_scale),
        grid=(b, s // ROW_TILE),
        in_specs=[_row_spec(d), _mod_spec(0, 1), _mod_spec(0, 0),
                  _whole_spec((d, d)), _whole_spec((d, d)), _whole_spec((d, d))],
        out_specs=[_row_spec(d), _row_spec(d), _vt_out_spec(1)],
        out_shape=[jax.ShapeDtypeStruct((b, s, d), BF16),
                   jax.ShapeDtypeStruct((b, s, d), BF16),
                   jax.ShapeDtypeStruct(_vt_shape(b, s, 1), BF16)],
        compiler_params=pltpu.CompilerParams(vmem_limit_bytes=VMEM_LIMIT),
        name="pre0",
    )(x, mod, mod, wq, wk, wvt)


def _pre1_kernel(x_ref, sc_ref, sh_ref, wd_ref, qn_ref, kvn_ref, wuq_ref, wuk_ref, wuvt_ref,
                 cos_ref, sin_ref, q_ref, k_ref, vt_ref, *, q_scale):
    u = (x_ref[0] * (1.0 + sc_ref[...]) + sh_ref[...]).astype(BF16)
    down = _dot(u, wd_ref[...])
    c_q = _rms_norm_rows(down[:, :MLA_Q_LORA], qn_ref[...]).astype(BF16)
    c_kv = _rms_norm_rows(down[:, MLA_Q_LORA:MLA_Q_LORA + MLA_KV_LORA], kvn_ref[...]).astype(BF16)
    k_rope = down[:, MLA_Q_LORA + MLA_KV_LORA:]
    q = _dot(c_q, wuq_ref[...])
    k_nope = _dot(c_kv, wuk_ref[...])
    _store_vt(vt_ref, _dot_nt(wuvt_ref[...], c_kv), 2)

    cos = cos_ref[0]
    sin = sin_ref[0]
    lane = lax.broadcasted_iota(jnp.int32, cos.shape, 1)
    first_half = lane < MLA_NOPE + MLA_ROPE // 2

    def rope(xh):
        partner = jnp.where(first_half,
                            pltpu.roll(xh, HEAD_LANES - MLA_ROPE // 2, 1),
                            pltpu.roll(xh, MLA_ROPE // 2, 1))
        return xh * cos + partner * sin

    k_rope = rope(k_rope)
    for h in range(MLA_HEADS):
        sl = slice(h * HEAD_LANES, (h + 1) * HEAD_LANES)
        q_ref[0, :, sl] = (rope(q[:, sl]) * q_scale).astype(BF16)
        k_ref[0, :, sl] = (k_nope[:, sl] + k_rope).astype(BF16)


def _pre1(x, mod, wd, qn, kvn, wuq, wuk, wuvt, cos_t, sin_t):
    b, s, d = x.shape
    hw = MLA_HEADS * HEAD_LANES
    q_scale = (MLA_NOPE + MLA_ROPE) ** -0.5 * LOG2E
    return pl.pallas_call(
        functools.partial(_pre1_kernel, q_scale=q_scale),
        grid=(b, s // ROW_TILE),
        in_specs=[_row_spec(d), _mod_spec(1, 1), _mod_spec(1, 0),
                  _whole_spec(wd.shape), _whole_spec(qn.shape), _whole_spec(kvn.shape),
                  _whole_spec(wuq.shape), _whole_spec(wuk.shape), _whole_spec(wuvt.shape),
                  _row_spec(HEAD_LANES), _row_spec(HEAD_LANES)],
        out_specs=[_row_spec(hw), _row_spec(hw), _vt_out_spec(2)],
        out_shape=[jax.ShapeDtypeStruct((b, s, hw), BF16),
                   jax.ShapeDtypeStruct((b, s, hw), BF16),
                   jax.ShapeDtypeStruct(_vt_shape(b, s, 2), BF16)],
        compiler_params=pltpu.CompilerParams(vmem_limit_bytes=VMEM_LIMIT),
        name="pre1",
    )(x, mod, mod, wd, qn, kvn, wuq, wuk, wuvt, cos_t, sin_t)


def _t5_bucket(rel):
    nb = REL_BUCKETS // 2
    max_exact = nb // 2
    n_log = nb - max_exact
    thresholds = []
    for k in range(1, n_log):
        n = max_exact
        while n ** n_log * max_exact ** k < REL_MAX_DIST ** k * max_exact ** n_log:
            n += 1
        thresholds.append(n)
    n = np.abs(rel)
    large = max_exact + sum((n >= thr).astype(np.int32) for thr in thresholds)
    return (rel > 0).astype(np.int32) * nb + np.where(n < max_exact, n, large)


def _near_bucket_tiles():
    kk = np.arange(ATTN_TILE, dtype=np.int32)[:, None]
    qq = np.arange(ATTN_TILE, dtype=np.int32)[None, :]
    diag = np.where(kk // CHUNK <= qq // CHUNK, _t5_bucket(kk - qq), -1)
    prev = _t5_bucket(kk - ATTN_TILE - qq)
    return jnp.asarray(np.stack([diag, prev]).astype(np.int32))


def _bias_kernel(tab_ref, idx_ref, o_ref):
    h = pl.program_id(0)
    idx = idx_ref[...]
    val = jnp.zeros(idx.shape, F32)
    for bkt in range(REL_BUCKETS):
        val = jnp.where(idx == bkt, tab_ref[bkt, h], val)
    far = tab_ref[REL_BUCKETS // 2 - 1, h]
    o_ref[0] = jnp.where(idx < 0, NEG_BIG, (val - far) * LOG2E)


def _near_bias(rel_table):
    idx = _near_bucket_tiles()
    return pl.pallas_call(
        _bias_kernel,
        grid=(DA_HEADS,),
        in_specs=[pl.BlockSpec(memory_space=pltpu.SMEM),
                  pl.BlockSpec(idx.shape, lambda h: (0, 0, 0))],
        out_specs=pl.BlockSpec((1,) + idx.shape, lambda h: (h, 0, 0, 0)),
        out_shape=jax.ShapeDtypeStruct((DA_HEADS,) + idx.shape, F32),
        name="near_bias",
    )(rel_table, idx)


def _near_mask():
    kk = jnp.arange(ATTN_TILE, dtype=jnp.int32)[:, None]
    qq = jnp.arange(ATTN_TILE, dtype=jnp.int32)[None, :]
    return jnp.where(kk // CHUNK <= qq // CHUNK, 0.0, NEG_BIG).astype(F32)[None, None]


def _tile_tables(nq, n_near_kinds):
    ti, tj, ta = [], [], []
    for kind in range(n_near_kinds):
        for i in range(kind, nq):
            ti.append(i), tj.append(i - kind), ta.append(kind)
    n_near = len(ti)
    for j in range(nq - n_near_kinds):
        for i in range(j + n_near_kinds, nq):
            ti.append(i), tj.append(j), ta.append(0)
    return np.asarray(ti, np.int32), np.asarray(tj, np.int32), np.asarray(ta, np.int32), n_near


def _attn_kernel(*refs, diff, lam_init, n_near, n_tiles, tables):
    ti_ref, tj_ref, ta_ref, q_ref, k_ref, vt_ref, near_ref = refs[:7]
    refs = refs[7:]
    if diff:
        lam_ref, subw_ref = refs[:2]
        refs = refs[2:]
    o_ref, s_buf, p_buf, mx_buf, al_buf, m_scr, acc = refs[:7]
    t = ATTN_TILE
    unroll = s_buf.shape[0]
    nq = acc.shape[1]
    acc_rows = acc.shape[2]
    dv = acc_rows - SUM_ROWS

    if diff:
        qm = refs[7]
        q_all = q_ref[0]
        lane = lax.broadcasted_iota(jnp.int32, q_all.shape, 1)
        zero = jnp.zeros_like(q_all)
        qm[0] = jnp.where(lane < DA_HEAD_DIM, q_all, zero)
        qm[1] = jnp.where(lane >= DA_HEAD_DIM, q_all, zero)

    m_scr[...] = jnp.full(m_scr.shape, NEG_BIG, F32)
    acc[...] = jnp.zeros(acc.shape, F32)

    def rows(idx):
        if isinstance(idx, int):
            return pl.ds(idx * t, t)
        return pl.ds(pl.multiple_of(idx * t, t), t)

    def lookup(table, ref, n):
        return int(table[n]) if isinstance(n, int) else ref[n]

    ti_tab, tj_tab, ta_tab = tables

    def q_map(mp, i):
        if diff:
            return qm[mp, rows(i), :]
        return q_ref[0, rows(i), mp * HEAD_LANES:(mp + 1) * HEAD_LANES]

    def score_dot(n, mp):
        i, j = lookup(ti_tab, ti_ref, n), lookup(tj_tab, tj_ref, n)
        k_m =k_ref[0, rows(j), :] if diff else k_ref[0, rows(j), mp * HEAD_LANES:(mp + 1) * HEAD_LANES]
        return _dot_nt(k_m, q_map(mp, i))

    def score_store(n, u, mp, s, near):
        s_buf[u, mp] = s
        if near:
            s = s_buf[u, mp] + near_ref[0, lookup(ta_tab, ta_ref, n)]
            s_buf[u, mp] = s
        mx_buf[u, mp] = jnp.broadcast_to(jnp.max(s, axis=0, keepdims=True), (SUBLANES, t))

    def per_query(x, stat):
        return x.reshape(x.shape[0] // SUBLANES, SUBLANES, t), stat[None]

    def exps(n, u, mp):
        i = lookup(ti_tab, ti_ref, n)
        m_old = m_scr[mp, i]
        m_new = jnp.maximum(m_old, mx_buf[u, mp])
        alpha = jnp.exp2(m_old - m_new)
        s3, m3 = per_query(s_buf[u, mp], m_new)
        p = jnp.exp2(s3 - m3).reshape(t, t)
        m_scr[mp, i] = m_new
        p_buf[u, mp] = p.astype(BF16)
        al_buf[u, mp] = alpha

    def pv_dot(n, u, mp):
        j = lookup(tj_tab, tj_ref, n)
        vt_m =vt_ref[0, 0, j] if diff else vt_ref[0, 0, j, mp * acc_rows:(mp + 1) * acc_rows, :]
        return _dot(vt_m, p_buf[u, mp]), al_buf[u, mp]

    def acc_update(n, mp, pv, alpha):
        i = lookup(ti_tab, ti_ref, n)
        a3, al3 = per_query(acc[mp, i], alpha)
        acc[mp, i] = (al3 * a3).reshape(acc_rows, t) + pv

    items = [(u, mp) for u in range(unroll) for mp in range(2)]

    def step(trip, near_flags, has1=True, has2=True, has3=True):
        pending = {}

        def issue(kk):
            u, mp = items[kk]
            pv = pv_dot(unroll * (trip - 2) + u, u, mp) if has3 else None
            s = score_dot(unroll * trip + u, mp) if has1 else None
            pending[kk] = (s, pv)

        def retire(kk):
            u, mp = items[kk]
            s, pv = pending.pop(kk)
            if has1:
                score_store(unroll * trip + u, u, mp, s, near_flags[u])
            if has3:
                acc_update(unroll * (trip - 2) + u, mp, *pv)

        for kk in range(min(DOT_LEAD, len(items))):
            issue(kk)
        for kk, (u, mp) in enumerate(items):
            if has2:
                exps(unroll * (trip - 1) + u, u, mp)
            if kk + DOT_LEAD < len(items):
                issue(kk + DOT_LEAD)
            if kk + DOT_LEAD - RETIRE_LAG >= 0:
                retire(kk + DOT_LEAD - RETIRE_LAG)
        for kk in sorted(pending):
            retire(kk)

    n_trips = n_tiles // unroll

    def near_flags(trip):
        return [unroll * trip + u < n_near for u in range(unroll)]

    def uniform(trip):
        flags = near_flags(trip)
        return all(flags) or not any(flags)

    step(0, near_flags(0), has2=False, has3=False)
    step(1, near_flags(1), has3=False)
    trip = 2
    while trip < n_trips:
        end = trip + 1
        while uniform(trip) and end < n_trips and near_flags(end) == near_flags(trip):
            end += 1
        if end - trip >= 2 and ATTN_LOOP_UNROLL > 0:
            flags = near_flags(trip)
            lax.fori_loop(trip, end, lambda tt, carry: (step(tt, flags), carry)[1], 0,
                          unroll=ATTN_LOOP_UNROLL)
        else:
            for tt in range(trip, end):
                step(tt, near_flags(tt))
        trip = end
    step(n_trips, None, has1=False)
    step(n_trips + 1, None, has1=False, has2=False)

    if diff:
        lam = (jnp.exp(jnp.sum(lam_ref[0:1] * lam_ref[1:2], axis=-1, keepdims=True))
               - jnp.exp(jnp.sum(lam_ref[2:3] * lam_ref[3:4], axis=-1, keepdims=True)) + lam_init)
        sub_w = subw_ref[...] * (1.0 - lam_init)

    def finalize(i, carry):
        o_a = acc[0, i, :dv] / acc[0, i, dv:dv + 1]
        o_b = acc[1, i, :dv] / acc[1, i, dv:dv + 1]
        if diff:
            o = o_a - lam * o_b
            o = o * lax.rsqrt(jnp.mean(o * o, axis=0, keepdims=True) + RMS_EPS) * sub_w
        else:
            o = jnp.concatenate([o_a, o_b], axis=0)
        o_ref[0, rows(i), :] = o.T.astype(BF16)
        return carry

    lax.fori_loop(0, nq, finalize, 0, unroll=4)


def _attention(q, k, vt, near, lam_vecs=None, subln_w=None, *, diff, lam_init=0.0):
    b, s, _ = q.shape
    t = ATTN_TILE
    nq = s // t
    heads = DA_HEADS
    width = HEAD_LANES if diff else 2 * HEAD_LANES
    acc_rows = (HEAD_LANES if diff else MLA_V) + SUM_ROWS
    assert vt.shape[3] == (1 if diff else 2) * acc_rows
    near_heads = near.shape[0]
    ti, tj, ta, n_near = _tile_tables(nq, near.shape[1])
    n_tiles = len(ti)
    assert n_tiles % ATTN_UNROLL == 0 and n_tiles // ATTN_UNROLL >= 2
    smem = pl.BlockSpec(memory_space=pltpu.SMEM)
    in_specs = [
        smem, smem, smem,
        pl.BlockSpec((1, s, width), lambda bi, h: (bi, 0, h)),
        pl.BlockSpec((1, s, width), lambda bi, h: (bi, 0, h)),
        pl.BlockSpec((1, 1, nq, vt.shape[3], t), lambda bi, h: (bi, h, 0, 0, 0)),
        pl.BlockSpec((1,) + near.shape[1:], lambda bi, h: (h if near_heads > 1 else 0, 0, 0, 0)),
    ]
    args = [jnp.asarray(ti), jnp.asarray(tj), jnp.asarray(ta), q, k, vt, near]
    if diff:
        in_specs += [pl.BlockSpec(lam_vecs.shape, lambda bi, h: (0, 0)),
                     pl.BlockSpec(subln_w.shape, lambda bi, h: (0, 0))]
        args += [lam_vecs, subln_w]
    un = ATTN_UNROLL
    scratch = [pltpu.VMEM((un, 2, t, t), F32), pltpu.VMEM((un, 2, t, t), BF16),
               pltpu.VMEM((un, 2, SUBLANES, t), F32), pltpu.VMEM((un, 2, SUBLANES, t), F32),
               pltpu.VMEM((2, nq, SUBLANES, t), F32), pltpu.VMEM((2, nq, acc_rows, t), F32)]
    if diff:
        scratch.append(pltpu.VMEM((2, s, HEAD_LANES), BF16))
    return pl.pallas_call(
        functools.partial(_attn_kernel, diff=diff, lam_init=lam_init, n_near=n_near, n_tiles=n_tiles,
                          tables=(ti, tj, ta)),
        grid=(b, heads),
        in_specs=in_specs,
        out_specs=pl.BlockSpec((1, s, HEAD_LANES), lambda bi, h: (bi, 0, h)),
        out_shape=jax.ShapeDtypeStruct((b, s, heads * HEAD_LANES), BF16),
        scratch_shapes=scratch,
        compiler_params=pltpu.CompilerParams(vmem_limit_bytes=VMEM_LIMIT),
        name="attn_diff" if diff else "attn_mla",
    )(*args)


def _post_ffn_kernel(o_ref, x_ref, g1_ref, wo_ref, ln1g_ref, ln1b_ref, sc_ref, sh_ref, g2_ref,
                     w1_ref, w2_ref, ln2g_ref, ln2b_ref, out_ref):
    y = _dot(o_ref[0], wo_ref[...])
    z = DEEPNORM_ALPHA * x_ref[0] + (1.0 + g1_ref[...]) * y
    x1 = _layer_norm_rows(z, ln1g_ref[...], ln1b_ref[...])
    u = (x1 * (1.0 + sc_ref[...]) + sh_ref[...]).astype(BF16)
    y = jnp.zeros(x1.shape, F32)
    for c in range(D_FF // FF_CHUNK):
        sl = slice(c * FF_CHUNK, (c + 1) * FF_CHUNK)
        h = jnp.maximum(_dot(u, w1_ref[:, sl]), 0.0)
        y = y + _dot((h * h).astype(BF16), w2_ref[sl, :])
    z = DEEPNORM_ALPHA * x1 + (1.0 + g2_ref[...]) * y
    out_ref[0] = _layer_norm_rows(z, ln2g_ref[...], ln2b_ref[...])


def _post_ffn(o, x, mod, layer, wo, w1, w2, ln_g, ln_b):
    b, s, d = x.shape
    resident = functools.partial(pl.BlockSpec, index_map=lambda bi, i: (0, 0), pipeline_mode=pl.Buffered(1))
    return pl.pallas_call(
        _post_ffn_kernel,
        grid=(b, s // ROW_TILE),
        in_specs=[_row_spec(d), _row_spec(d), _mod_spec(layer, 2), resident((d, d)),
                  _whole_spec((1, d)), _whole_spec((1, d)), _mod_spec(layer, 4), _mod_spec(layer, 3),
                  _mod_spec(layer, 5), resident((d, D_FF)), resident((D_FF, d)),
                  _whole_spec((1, d)), _whole_spec((1, d))],
        out_specs=_row_spec(d),
        out_shape=jax.ShapeDtypeStruct((b, s, d), F32),
        compiler_params=pltpu.CompilerParams(vmem_limit_bytes=VMEM_LIMIT),
        name=f"post_ffn{layer}",
    )(o, x, mod, wo, ln_g[0][None], ln_b[0][None], mod, mod, mod, w1, w2, ln_g[1][None], ln_b[1][None])


def _rope_tables(pos_offset, s):
    half = MLA_ROPE // 2
    pos = pos_offset[:, None] + jnp.arange(s, dtype=jnp.int32)[None, :]
    inv = ROPE_THETA ** (-jnp.arange(half, dtype=F32) / half)
    ang = pos.astype(F32)[..., None] * inv
    cos = jnp.cos(ang)
    sin = jnp.sin(ang)
    b = pos.shape[0]
    ones = jnp.ones((b, s, MLA_NOPE), F32)
    zeros_tail = jnp.zeros((b, s, HEAD_LANES - MLA_NOPE - MLA_ROPE), F32)
    cos_t = jnp.concatenate([ones, cos, cos, zeros_tail], axis=-1)
    sin_t = jnp.concatenate([jnp.zeros_like(ones), -sin, sin, zeros_tail], axis=-1)
    return cos_t, sin_t


def _pad_heads(w, per_head, lead):
    w = w.reshape(lead, MLA_HEADS, per_head)
    w = jnp.pad(w, ((0, 0), (0, 0), (0, HEAD_LANES - per_head)))
    return w.reshape(lead, MLA_HEADS * HEAD_LANES)


def kernel(x, c, pos_offset, ada_w, ada_b, ln_g, ln_b, rel_table, da_w_qkv, da_w_o, da_lam_q1, da_lam_k1, da_lam_q2, da_lam_k2, da_subln_w, mla_w_down, mla_q_norm_w, mla_w_uq, mla_kv_norm_w, mla_w_ukv, mla_w_o, ffn_w1, ffn_w2):
    b, s, d = x.shape
    assert d == D_MODEL and s % ROW_TILE == 0 and ROW_TILE % ATTN_TILE == 0 and ATTN_TILE % CHUNK == 0

    mod = _modulation(c, ada_w, ada_b)

    w_qkv = da_w_qkv[0].astype(BF16)
    q, k, vt = _pre0(x, mod, w_qkv[:, :d], w_qkv[:, d:2 * d], w_qkv[:, 2 * d:].T)
    lam_vecs = jnp.stack([da_lam_q1[0], da_lam_k1[0], da_lam_q2[0], da_lam_k2[0]])
    lam_init = 0.8 - 0.6 * math.exp(-0.3 * 0)
    o = _attention(q, k, vt, _near_bias(rel_table), lam_vecs, da_subln_w[0].reshape(-1, 1),
                   diff=True, lam_init=lam_init)
    x = _post_ffn(o, x, mod, 0, da_w_o[0].astype(BF16), ffn_w1[0].astype(BF16), ffn_w2[0].astype(BF16),
                  ln_g[0], ln_b[0])

    n_lat = MLA_Q_LORA + MLA_KV_LORA
    w_down = mla_w_down[0]
    w_down = jnp.concatenate(
        [w_down[:, :n_lat], jnp.zeros((d, MLA_NOPE), F32), w_down[:, n_lat:],
         jnp.zeros((d, HEAD_LANES - MLA_NOPE - MLA_ROPE), F32)], axis=-1).astype(BF16)
    w_uq = _pad_heads(mla_w_uq[0], MLA_NOPE + MLA_ROPE, MLA_Q_LORA).astype(BF16)
    w_ukv = mla_w_ukv[0].reshape(MLA_KV_LORA, MLA_HEADS, MLA_NOPE + MLA_V)
    w_uk = _pad_heads(w_ukv[..., :MLA_NOPE].reshape(MLA_KV_LORA, -1), MLA_NOPE, MLA_KV_LORA).astype(BF16)
    w_uvt = w_ukv[..., MLA_NOPE:].reshape(MLA_KV_LORA, -1).T.astype(BF16)
    cos_t, sin_t = _rope_tables(pos_offset, s)
    q, k, vt = _pre1(x, mod, w_down, mla_q_norm_w[0][None], mla_kv_norm_w[0][None],
                     w_uq, w_uk, w_uvt, cos_t, sin_t)
    o = _attention(q, k, vt, _near_mask(), diff=False)
    return _post_ffn(o, x, mod, 1, mla_w_o[0].astype(BF16), ffn_w1[1].astype(BF16), ffn_w2[1].astype(BF16),
                     ln_g[1], ln_b[1])
```

```python
import functools
import math

import jax
import jax.numpy as jnp
import numpy as np
from jax import lax
from jax.experimental import pallas as pl
from jax.experimental.pallas import tpu as pltpu

F32 = jnp.float32
BF16 = jnp.bfloat16

D_MODEL = 1024
DEPTH = 2
CHUNK = 64
DA_HEAD_DIM = 64
DA_HEADS = D_MODEL // (2 * DA_HEAD_DIM)
MLA_HEADS = 16
MLA_NOPE = 64
MLA_ROPE = 32
MLA_V = 64
MLA_Q_LORA = 512
MLA_KV_LORA = 256
ROPE_THETA = 10000.0
REL_BUCKETS = 32
REL_MAX_DIST = 128
D_FF = 4 * D_MODEL
DEEPNORM_ALPHA = (2 * DEPTH) ** 0.25
LN_EPS = 1e-5
RMS_EPS = 1e-6

LANES = 128
SUBLANES = 8
HEAD_LANES = 128
ATTN_TILE = 256
ATTN_UNROLL = 2
DOT_LEAD = 2
RETIRE_LAG = 3
SUM_ROWS = 16
ROW_TILE = 1024
FF_CHUNK = 1024
LOG2E = math.log2(math.e)
NEG_BIG = -1e30
VMEM_LIMIT = 56 * 1024 * 1024

_NT = (((1,), (1,)), ((), ()))


def _dot(a, b):
    return jnp.dot(a, b, preferred_element_type=F32)


def _dot_nt(a, b):
    return lax.dot_general(a, b, _NT, preferred_element_type=F32)


def _layer_norm_rows(z, g, b):
    mu = jnp.mean(z, axis=-1, keepdims=True)
    zc = z - mu
    var = jnp.mean(zc * zc, axis=-1, keepdims=True)
    return zc * lax.rsqrt(var + LN_EPS) * g + b


def _rms_norm_rows(z, w):
    return z * lax.rsqrt(jnp.mean(z * z, axis=-1, keepdims=True) + RMS_EPS) * w


def _mod_kernel(c_ref, w_ref, b_ref, o_ref):
    c = c_ref[...]
    c_act = c * jax.nn.sigmoid(c)
    o_ref[0, 0] = _dot(c_act.astype(BF16), w_ref[0].astype(BF16)) + b_ref[0, 0]


def _modulation(c, ada_w, ada_b):
    b = c.shape[0]
    rows = 8
    c_pad = jnp.zeros((rows, D_MODEL), F32).at[:b].set(c)
    out = pl.pallas_call(
        _mod_kernel,
        grid=(DEPTH, 6),
        in_specs=[
            pl.BlockSpec((rows, D_MODEL), lambda i, j: (0, 0)),
            pl.BlockSpec((1, D_MODEL, D_MODEL), lambda i, j: (i, 0, j)),
            pl.BlockSpec((1, 1, 1, D_MODEL), lambda i, j: (i, j, 0, 0)),
        ],
        out_specs=pl.BlockSpec((1, 1, rows, D_MODEL), lambda i, j: (i, j, 0, 0)),
        out_shape=jax.ShapeDtypeStruct((DEPTH, 6, rows, D_MODEL), F32),
        compiler_params=pltpu.CompilerParams(vmem_limit_bytes=VMEM_LIMIT),
        name="mod",
    )(c_pad, ada_w, ada_b.reshape(DEPTH, 6, 1, D_MODEL))
    return out[:, :, :b].reshape(DEPTH, 6, b, 1, D_MODEL)


def _mod_spec(layer, idx):
    return pl.BlockSpec((None, None, None, 1, D_MODEL), lambda b, i: (layer, idx, b, 0, 0))


def _row_spec(width):
    return pl.BlockSpec((1, ROW_TILE, width), lambda b, i: (b, i, 0))


def _whole_spec(shape):
    return pl.BlockSpec(shape, lambda b, i: (0,) * len(shape))


def _vt_rows(maps_per_head):
    return HEAD_LANES + maps_per_head * SUM_ROWS


def _vt_shape(b, s, maps_per_head):
    return (b, DA_HEADS, s // ATTN_TILE, _vt_rows(maps_per_head), ATTN_TILE)


def _vt_out_spec(maps_per_head):
    return pl.BlockSpec((1, DA_HEADS, ROW_TILE // ATTN_TILE, _vt_rows(maps_per_head), ATTN_TILE),
                        lambda b, i: (b, 0, i, 0, 0))


def _store_vt(vt_ref, vt, maps_per_head):
    groups = DA_HEADS * maps_per_head
    vt = vt.astype(BF16).reshape(groups, HEAD_LANES // maps_per_head, ROW_TILE)
    ones = jnp.ones((groups, SUM_ROWS, ROW_TILE), BF16)
    vt = jnp.concatenate([vt, ones], axis=1).reshape(DA_HEADS, _vt_rows(maps_per_head), ROW_TILE)
    for t in range(ROW_TILE // ATTN_TILE):
        vt_ref[0, :, t] = vt[:, :, t * ATTN_TILE:(t + 1) * ATTN_TILE]


def _pre0_kernel(x_ref, sc_ref, sh_ref, wq_ref, wk_ref, wvt_ref, q_ref, k_ref, vt_ref, *, q_scale):
    u = (x_ref[0] * (1.0 + sc_ref[...]) + sh_ref[...]).astype(BF16)
    q_ref[0] = (_dot(u, wq_ref[...]) * q_scale).astype(BF16)
    k_ref[0] = _dot(u, wk_ref[...]).astype(BF16)
    _store_vt(vt_ref, _dot_nt(wvt_ref[...], u), 1)


def _pre0(x, mod, wq, wk, wvt):
    b, s, d = x.shape
    q_scale = DA_HEAD_DIM ** -0.5 * LOG2E
    return pl.pallas_call(
        functools.partial(_pre0_kernel, q_scale=q_scale),
        grid=(b, s // ROW_TILE),
        in_specs=[_row_spec(d), _mod_spec(0, 1), _mod_spec(0, 0),
                  _whole_spec((d, d)), _whole_spec((d, d)), _whole_spec((d, d))],
        out_specs=[_row_spec(d), _row_spec(d), _vt_out_spec(1)],
        out_shape=[jax.ShapeDtypeStruct((b, s, d), BF16),
                   jax.ShapeDtypeStruct((b, s, d), BF16),
                   jax.ShapeDtypeStruct(_vt_shape(b, s, 1), BF16)],
        compiler_params=pltpu.CompilerParams(vmem_limit_bytes=VMEM_LIMIT),
        name="pre0",
    )(x, mod, mod, wq, wk, wvt)


def _pre1_kernel(x_ref, sc_ref, sh_ref, wd_ref, qn_ref, kvn_ref, wuq_ref, wuk_ref, wuvt_ref,
                 cos_ref, sin_ref, q_ref, k_ref, vt_ref, *, q_scale):
    u = (x_ref[0] * (1.0 + sc_ref[...]) + sh_ref[...]).astype(BF16)
    down = _dot(u, wd_ref[...])
    c_q = _rms_norm_rows(down[:, :MLA_Q_LORA], qn_ref[...]).astype(BF16)
    c_kv = _rms_norm_rows(down[:, MLA_Q_LORA:MLA_Q_LORA + MLA_KV_LORA], kvn_ref[...]).astype(BF16)
    k_rope = down[:, MLA_Q_LORA + MLA_KV_LORA:]
    q = _dot(c_q, wuq_ref[...])
    k_nope = _dot(c_kv, wuk_ref[...])
    _store_vt(vt_ref, _dot_nt(wuvt_ref[...], c_kv), 2)

    cos = cos_ref[0]
    sin = sin_ref[0]
    lane = lax.broadcasted_iota(jnp.int32, cos.shape, 1)
    first_half = lane < MLA_NOPE + MLA_ROPE // 2

    def rope(xh):
        partner = jnp.where(first_half,
                            pltpu.roll(xh, HEAD_LANES - MLA_ROPE // 2, 1),
                            pltpu.roll(xh, MLA_ROPE // 2, 1))
        return xh * cos + partner * sin

    k_rope = rope(k_rope)
    for h in range(MLA_HEADS):
        sl = slice(h * HEAD_LANES, (h + 1) * HEAD_LANES)
        q_ref[0, :, sl] = (rope(q[:, sl]) * q_scale).astype(BF16)
        k_ref[0, :, sl] = (k_nope[:, sl] + k_rope).astype(BF16)


def _pre1(x, mod, wd, qn, kvn, wuq, wuk, wuvt, cos_t, sin_t):
    b, s, d = x.shape
    hw = MLA_HEADS * HEAD_LANES
    q_scale = (MLA_NOPE + MLA_ROPE) ** -0.5 * LOG2E
    return pl.pallas_call(
        functools.partial(_pre1_kernel, q_scale=q_scale),
        grid=(b, s // ROW_TILE),
        in_specs=[_row_spec(d), _mod_spec(1, 1), _mod_spec(1, 0),
                  _whole_spec(wd.shape), _whole_spec(qn.shape), _whole_spec(kvn.shape),
                  _whole_spec(wuq.shape), _whole_spec(wuk.shape), _whole_spec(wuvt.shape),
                  _row_spec(HEAD_LANES), _row_spec(HEAD_LANES)],
        out_specs=[_row_spec(hw), _row_spec(hw), _vt_out_spec(2)],
        out_shape=[jax.ShapeDtypeStruct((b, s, hw), BF16),
                   jax.ShapeDtypeStruct((b, s, hw), BF16),
                   jax.ShapeDtypeStruct(_vt_shape(b, s, 2), BF16)],
        compiler_params=pltpu.CompilerParams(vmem_limit_bytes=VMEM_LIMIT),
        name="pre1",
    )(x, mod, mod, wd, qn, kvn, wuq, wuk, wuvt, cos_t, sin_t)


def _t5_bucket(rel):
    nb = REL_BUCKETS // 2
    max_exact = nb // 2
    n_log = nb - max_exact
    thresholds = []
    for k in range(1, n_log):
        n = max_exact
        while n ** n_log * max_exact ** k < REL_MAX_DIST ** k * max_exact ** n_log:
            n += 1
        thresholds.append(n)
    n = np.abs(rel)
    large = max_exact + sum((n >= thr).astype(np.int32) for thr in thresholds)
    return (rel > 0).astype(np.int32) * nb + np.where(n < max_exact, n, large)


def _near_bucket_tiles():
    kk = np.arange(ATTN_TILE, dtype=np.int32)[:, None]
    qq = np.arange(ATTN_TILE, dtype=np.int32)[None, :]
    diag = np.where(kk // CHUNK <= qq // CHUNK, _t5_bucket(kk - qq), -1)
    prev = _t5_bucket(kk - ATTN_TILE - qq)
    return jnp.asarray(np.stack([diag, prev]).astype(np.int32))


def _bias_kernel(tab_ref, idx_ref, o_ref):
    h = pl.program_id(0)
    idx = idx_ref[...]
    val = jnp.zeros(idx.shape, F32)
    for bkt in range(REL_BUCKETS):
        val = jnp.where(idx == bkt, tab_ref[bkt, h], val)
    far = tab_ref[REL_BUCKETS // 2 - 1, h]
    o_ref[0] = jnp.where(idx < 0, NEG_BIG, (val - far) * LOG2E)


def _near_bias(rel_table):
    idx = _near_bucket_tiles()
    return pl.pallas_call(
        _bias_kernel,
        grid=(DA_HEADS,),
        in_specs=[pl.BlockSpec(memory_space=pltpu.SMEM),
                  pl.BlockSpec(idx.shape, lambda h: (0, 0, 0))],
        out_specs=pl.BlockSpec((1,) + idx.shape, lambda h: (h, 0, 0, 0)),
        out_shape=jax.ShapeDtypeStruct((DA_HEADS,) + idx.shape, F32),
        name="near_bias",
    )(rel_table, idx)


def _near_mask():
    kk = jnp.arange(ATTN_TILE, dtype=jnp.int32)[:, None]
    qq = jnp.arange(ATTN_TILE, dtype=jnp.int32)[None, :]
    return jnp.where(kk // CHUNK <= qq // CHUNK, 0.0, NEG_BIG).astype(F32)[None, None]


def _tile_tables(nq, n_near_kinds):
    ti, tj, ta = [], [], []
    for kind in range(n_near_kinds):
        for i in range(kind, nq):
            ti.append(i), tj.append(i - kind), ta.append(kind)
    n_near = len(ti)
    for j in range(nq - n_near_kinds):
        for i in range(j + n_near_kinds, nq):
            ti.append(i), tj.append(j), ta.append(0)
    return np.asarray(ti, np.int32), np.asarray(tj, np.int32), np.asarray(ta, np.int32), n_near


def _attn_kernel(*refs, diff, lam_init, n_near, n_tiles, tables):
    q_ref, k_ref, vt_ref, near_ref = refs[:4]
    refs = refs[4:]
    if diff:
        lam_ref, subw_ref = refs[:2]
        refs = refs[2:]
    o_ref, s_buf, p_buf, mx_buf, al_buf, m_scr, acc = refs[:7]
    t = ATTN_TILE
    unroll = s_buf.shape[0]
    nq = acc.shape[1]
    acc_rows = acc.shape[2]
    dv = acc_rows - SUM_ROWS

    if diff:
        qm = refs[7]
        q_all = q_ref[0]
        lane = lax.broadcasted_iota(jnp.int32, q_all.shape, 1)
        zero = jnp.zeros_like(q_all)
        qm[0] = jnp.where(lane < DA_HEAD_DIM, q_all, zero)
        qm[1] = jnp.where(lane >= DA_HEAD_DIM, q_all, zero)

        lam = (jnp.exp(jnp.sum(lam_ref[0:1] * lam_ref[1:2], axis=-1, keepdims=True))
               - jnp.exp(jnp.sum(lam_ref[2:3] * lam_ref[3:4], axis=-1, keepdims=True)) + lam_init)
        sub_w = subw_ref[...] * (1.0 - lam_init)

    ti_tab, tj_tab, ta_tab = ([int(v) for v in tab] for tab in tables)

    def rows(idx):
        return pl.ds(idx * t, t)

    def first_tile(n):
        return n < nq

    def q_map(mp, i):
        if diff:
            return qm[mp, rows(i), :]
        return q_ref[0, rows(i), mp * HEAD_LANES:(mp + 1) * HEAD_LANES]

    def score_dot(n, mp):
        i, j = ti_tab[n], tj_tab[n]
        k_m = k_ref[0, rows(j), :] if diff else k_ref[0, rows(j), mp * HEAD_LANES:(mp + 1) * HEAD_LANES]
        return _dot_nt(k_m, q_map(mp, i))

    def score_store(n, u, mp, s, near):
        s_buf[u, mp] = s
        if near:
            s = s_buf[u, mp] + near_ref[0, ta_tab[n]]
            s_buf[u, mp] = s
        mx_buf[u, mp] = jnp.broadcast_to(jnp.max(s, axis=0, keepdims=True), (SUBLANES, t))

    def per_query(x, stat):
        return x.reshape(x.shape[0] // SUBLANES, SUBLANES, t), stat[None]

    def exps(n, u, mp):
        i = ti_tab[n]
        if first_tile(n):
            m_new = mx_buf[u, mp]
        else:
            m_old = m_scr[mp, i]
            m_new = jnp.maximum(m_old, mx_buf[u, mp])
            al_buf[u, mp] = jnp.exp2(m_old - m_new)
        s3, m3 = per_query(s_buf[u, mp], m_new)
        p = jnp.exp2(s3 - m3).reshape(t, t)
        m_scr[mp, i] = m_new
        p_buf[u, mp] = p.astype(BF16)

    def pv_dot(n, u, mp):
        j = tj_tab[n]
        vt_m = vt_ref[0, 0, j] if diff else vt_ref[0, 0, j, mp * acc_rows:(mp + 1) * acc_rows, :]
        alpha = None if first_tile(n) else al_buf[u, mp]
        return _dot(vt_m, p_buf[u, mp]), alpha

    def acc_update(n, mp, pv, alpha):
        i = ti_tab[n]
        if first_tile(n):
            acc[mp, i] = pv
        else:
            a3, al3 = per_query(acc[mp, i], alpha)
            acc[mp, i] = (al3 * a3).reshape(acc_rows, t) + pv

    items = [(u, mp) for u in range(unroll) for mp in range(2)]

    def step(trip, near_flags, has1=True, has2=True, has3=True):
        pending = {}

        def issue(kk):
            u, mp = items[kk]
            pv = pv_dot(unroll * (trip - 2) + u, u, mp) if has3 else None
            s = score_dot(unroll * trip + u, mp) if has1 else None
            pending[kk] = (s, pv)

        def retire(kk):
            u, mp = items[kk]
            s, pv = pending.pop(kk)
            if has1:
                score_store(unroll * trip + u, u, mp, s, near_flags[u])
            if has3:
                acc_update(unroll * (trip - 2) + u, mp, *pv)

        for kk in range(min(DOT_LEAD, len(items))):
            issue(kk)
        for kk, (u, mp) in enumerate(items):
            if has2:
                exps(unroll * (trip - 1) + u, u, mp)
            if kk + DOT_LEAD < len(items):
                issue(kk + DOT_LEAD)
            if kk + DOT_LEAD - RETIRE_LAG >= 0:
                retire(kk + DOT_LEAD - RETIRE_LAG)
        for kk in sorted(pending):
            retire(kk)

    def finalize(i):
        o_a = acc[0, i, :dv] / acc[0, i, dv:dv + 1]
        o_b = acc[1, i, :dv] / acc[1, i, dv:dv + 1]
        if diff:
            o = o_a - lam * o_b
            o = o * lax.rsqrt(jnp.mean(o * o, axis=0, keepdims=True) + RMS_EPS) * sub_w
        else:
            o = jnp.concatenate([o_a, o_b], axis=0)
        o_ref[0, rows(i), :] = o.T.astype(BF16)

    n_trips = n_tiles // unroll
    last_step = {i: n // unroll + 2 for n, i in enumerate(ti_tab)}
    for trip in range(n_trips + 2):
        flags = [unroll * trip + u < n_near for u in range(unroll)]
        step(trip, flags, has1=trip < n_trips, has2=1 <= trip <= n_trips, has3=trip >= 2)
        for i in sorted(i for i, last in last_step.items() if last == trip):
            finalize(i)


def _attention(q, k, vt, near, lam_vecs=None, subln_w=None, *, diff, lam_init=0.0):
    b, s, _ = q.shape
    t = ATTN_TILE
    nq = s // t
    heads = DA_HEADS
    width = HEAD_LANES if diff else 2 * HEAD_LANES
    acc_rows = (HEAD_LANES if diff else MLA_V) + SUM_ROWS
    assert vt.shape[3] == (1 if diff else 2) * acc_rows
    near_heads = near.shape[0]
    ti, tj, ta, n_near = _tile_tables(nq, near.shape[1])
    n_tiles = len(ti)
    assert n_tiles % ATTN_UNROLL == 0 and n_tiles // ATTN_UNROLL >= 2
    in_specs = [
        pl.BlockSpec((1, s, width), lambda bi, h: (bi, 0, h)),
        pl.BlockSpec((1, s, width), lambda bi, h: (bi, 0, h)),
        pl.BlockSpec((1, 1, nq, vt.shape[3], t), lambda bi, h: (bi, h, 0, 0, 0)),
        pl.BlockSpec((1,) + near.shape[1:], lambda bi, h: (h if near_heads > 1 else 0, 0, 0, 0)),
    ]
    args = [q, k, vt, near]
    if diff:
        in_specs += [pl.BlockSpec(lam_vecs.shape, lambda bi, h: (0, 0)),
                     pl.BlockSpec(subln_w.shape, lambda bi, h: (0, 0))]
        args += [lam_vecs, subln_w]
    un = ATTN_UNROLL
    scratch = [pltpu.VMEM((un, 2, t, t), F32), pltpu.VMEM((un, 2, t, t), BF16),
               pltpu.VMEM((un, 2, SUBLANES, t), F32), pltpu.VMEM((un, 2, SUBLANES, t), F32),
               pltpu.VMEM((2, nq, SUBLANES, t), F32), pltpu.VMEM((2, nq, acc_rows, t), F32)]
    if diff:
        scratch.append(pltpu.VMEM((2, s, HEAD_LANES), BF16))
    return pl.pallas_call(
        functools.partial(_attn_kernel, diff=diff, lam_init=lam_init, n_near=n_near, n_tiles=n_tiles,
                          tables=(ti, tj, ta)),
        grid=(b, heads),
        in_specs=in_specs,
        out_specs=pl.BlockSpec((1, s, HEAD_LANES), lambda bi, h: (bi, 0, h)),
        out_shape=jax.ShapeDtypeStruct((b, s, heads * HEAD_LANES), BF16),
        scratch_shapes=scratch,
        compiler_params=pltpu.CompilerParams(vmem_limit_bytes=VMEM_LIMIT),
        name="attn_diff" if diff else "attn_mla",
    )(*args)


def _post_ffn_kernel(o_ref, x_ref, g1_ref, wo_ref, ln1g_ref, ln1b_ref, sc_ref, sh_ref, g2_ref,
                     w1_ref, w2_ref, ln2g_ref, ln2b_ref, out_ref):
    y = _dot(o_ref[0], wo_ref[...])
    z = DEEPNORM_ALPHA * x_ref[0] + (1.0 + g1_ref[...]) * y
    x1 = _layer_norm_rows(z, ln1g_ref[...], ln1b_ref[...])
    u = (x1 * (1.0 + sc_ref[...]) + sh_ref[...]).astype(BF16)
    y = jnp.zeros(x1.shape, F32)
    for c in range(D_FF // FF_CHUNK):
        sl = slice(c * FF_CHUNK, (c + 1) * FF_CHUNK)
        h = jnp.maximum(_dot(u, w1_ref[:, sl]), 0.0)
        y = y + _dot((h * h).astype(BF16), w2_ref[sl, :])
    z = DEEPNORM_ALPHA * x1 + (1.0 + g2_ref[...]) * y
    out_ref[0] = _layer_norm_rows(z, ln2g_ref[...], ln2b_ref[...])


def _post_ffn(o, x, mod, layer, wo, w1, w2, ln_g, ln_b):
    b, s, d = x.shape
    resident = functools.partial(pl.BlockSpec, index_map=lambda bi, i: (0, 0), pipeline_mode=pl.Buffered(1))
    return pl.pallas_call(
        _post_ffn_kernel,
        grid=(b, s // ROW_TILE),
        in_specs=[_row_spec(d), _row_spec(d), _mod_spec(layer, 2), resident((d, d)),
                  _whole_spec((1, d)), _whole_spec((1, d)), _mod_spec(layer, 4), _mod_spec(layer, 3),
                  _mod_spec(layer, 5), resident((d, D_FF)), resident((D_FF, d)),
                  _whole_spec((1, d)), _whole_spec((1, d))],
        out_specs=_row_spec(d),
        out_shape=jax.ShapeDtypeStruct((b, s, d), F32),
        compiler_params=pltpu.CompilerParams(vmem_limit_bytes=VMEM_LIMIT),
        name=f"post_ffn{layer}",
    )(o, x, mod, wo, ln_g[0][None], ln_b[0][None], mod, mod, mod, w1, w2, ln_g[1][None], ln_b[1][None])


def _rope_tables(pos_offset, s):
    half = MLA_ROPE // 2
    pos = pos_offset[:, None] + jnp.arange(s, dtype=jnp.int32)[None, :]
    inv = ROPE_THETA ** (-jnp.arange(half, dtype=F32) / half)
    ang = pos.astype(F32)[..., None] * inv
    cos = jnp.cos(ang)
    sin = jnp.sin(ang)
    b = pos.shape[0]
    ones = jnp.ones((b, s, MLA_NOPE), F32)
    zeros_tail = jnp.zeros((b, s, HEAD_LANES - MLA_NOPE - MLA_ROPE), F32)
    cos_t = jnp.concatenate([ones, cos, cos, zeros_tail], axis=-1)
    sin_t = jnp.concatenate([jnp.zeros_like(ones), -sin, sin, zeros_tail], axis=-1)
    return cos_t, sin_t


def _pad_heads(w, per_head, lead):
    w = w.reshape(lead, MLA_HEADS, per_head)
    w = jnp.pad(w, ((0, 0), (0, 0), (0, HEAD_LANES - per_head)))
    return w.reshape(lead, MLA_HEADS * HEAD_LANES)


def kernel(x, c, pos_offset, ada_w, ada_b, ln_g, ln_b, rel_table, da_w_qkv, da_w_o, da_lam_q1, da_lam_k1, da_lam_q2, da_lam_k2, da_subln_w, mla_w_down, mla_q_norm_w, mla_w_uq, mla_kv_norm_w, mla_w_ukv, mla_w_o, ffn_w1, ffn_w2):
    b, s, d = x.shape
    assert d == D_MODEL and s % ROW_TILE == 0 and ROW_TILE % ATTN_TILE == 0 and ATTN_TILE % CHUNK == 0

    mod = _modulation(c, ada_w, ada_b)

    w_qkv = da_w_qkv[0].astype(BF16)
    q, k, vt = _pre0(x, mod, w_qkv[:, :d], w_qkv[:, d:2 * d], w_qkv[:, 2 * d:].T)
    lam_vecs = jnp.stack([da_lam_q1[0], da_lam_k1[0], da_lam_q2[0], da_lam_k2[0]])
    lam_init = 0.8 - 0.6 * math.exp(-0.3 * 0)
    o = _attention(q, k, vt, _near_bias(rel_table), lam_vecs, da_subln_w[0].reshape(-1, 1),
                   diff=True, lam_init=lam_init)
    x = _post_ffn(o, x, mod, 0, da_w_o[0].astype(BF16), ffn_w1[0].astype(BF16), ffn_w2[0].astype(BF16),
                  ln_g[0], ln_b[0])

    n_lat = MLA_Q_LORA + MLA_KV_LORA
    w_down = mla_w_down[0]
    w_down = jnp.concatenate(
        [w_down[:, :n_lat], jnp.zeros((d, MLA_NOPE), F32), w_down[:, n_lat:],
         jnp.zeros((d, HEAD_LANES - MLA_NOPE - MLA_ROPE), F32)], axis=-1).astype(BF16)
    w_uq = _pad_heads(mla_w_uq[0], MLA_NOPE + MLA_ROPE, MLA_Q_LORA).astype(BF16)
    w_ukv = mla_w_ukv[0].reshape(MLA_KV_LORA, MLA_HEADS, MLA_NOPE + MLA_V)
    w_uk = _pad_heads(w_ukv[..., :MLA_NOPE].reshape(MLA_KV_LORA, -1), MLA_NOPE, MLA_KV_LORA).astype(BF16)
    w_uvt = w_ukv[..., MLA_NOPE:].reshape(MLA_KV_LORA, -1).T.astype(BF16)
    cos_t, sin_t = _rope_tables(pos_offset, s)
    q, k, vt = _pre1(x, mod, w_down, mla_q_norm_w[0][None], mla_kv_norm_w[0][None],
                     w_uq, w_uk, w_uvt, cos_t, sin_t)
    o = _attention(q, k, vt, _near_mask(), diff=False)
    return _post_ffn(o, x, mod, 1, mla_w_o[0].astype(BF16), ffn_w1[1].astype(BF16), ffn_w2[1].astype(BF16),
                     ln_g[1], ln_b[1])
```

```python
import functools
import math

import jax
import jax.numpy as jnp
import numpy as np
from jax import lax
from jax.experimental import pallas as pl
from jax.experimental.pallas import tpu as pltpu

F32 = jnp.float32
BF16 = jnp.bfloat16

D_MODEL = 1024
DEPTH = 2
CHUNK = 64
DA_HEAD_DIM = 64
DA_HEADS = D_MODEL // (2 * DA_HEAD_DIM)
MLA_HEADS = 16
MLA_NOPE = 64
MLA_ROPE = 32
MLA_V = 64
MLA_Q_LORA = 512
MLA_KV_LORA = 256
ROPE_THETA = 10000.0
REL_BUCKETS = 32
REL_MAX_DIST = 128
D_FF = 4 * D_MODEL
DEEPNORM_ALPHA = (2 * DEPTH) ** 0.25
LN_EPS = 1e-5
RMS_EPS = 1e-6

LANES = 128
SUBLANES = 8
HEAD_LANES = 128
ATTN_TILE = 256
ATTN_UNROLL = 2
DOT_LEAD = 2
RETIRE_LAG = 3
SUM_ROWS = 16
ROW_TILE = 1024
FF_CHUNK = 1024
LOG2E = math.log2(math.e)
NEG_BIG = -1e30
VMEM_LIMIT = 56 * 1024 * 1024

_NT = (((1,), (1,)), ((), ()))


def _dot(a, b):
    return jnp.dot(a, b, preferred_element_type=F32)


def _dot_nt(a, b):
    return lax.dot_general(a, b, _NT, preferred_element_type=F32)


def _layer_norm_rows(z, g, b):
    mu = jnp.mean(z, axis=-1, keepdims=True)
    zc = z - mu
    var = jnp.mean(zc * zc, axis=-1, keepdims=True)
    return zc * lax.rsqrt(var + LN_EPS) * g + b


def _rms_norm_rows(z, w):
    return z * lax.rsqrt(jnp.mean(z * z, axis=-1, keepdims=True) + RMS_EPS) * w


def _mod_kernel(c_ref, w_ref, b_ref, o_ref):
    c = c_ref[...]
    c_act = c * jax.nn.sigmoid(c)
    o_ref[0, 0] = _dot(c_act.astype(BF16), w_ref[0].astype(BF16)) + b_ref[0, 0]


def _modulation(c, ada_w, ada_b):
    b = c.shape[0]
    rows = 8
    c_pad = jnp.zeros((rows, D_MODEL), F32).at[:b].set(c)
    out = pl.pallas_call(
        _mod_kernel,
        grid=(DEPTH, 6),
        in_specs=[
            pl.BlockSpec((rows, D_MODEL), lambda i, j: (0, 0)),
            pl.BlockSpec((1, D_MODEL, D_MODEL), lambda i, j: (i, 0, j)),
            pl.BlockSpec((1, 1, 1, D_MODEL), lambda i, j: (i, j, 0, 0)),
        ],
        out_specs=pl.BlockSpec((1, 1, rows, D_MODEL), lambda i, j: (i, j, 0, 0)),
        out_shape=jax.ShapeDtypeStruct((DEPTH, 6, rows, D_MODEL), F32),
        compiler_params=pltpu.CompilerParams(vmem_limit_bytes=VMEM_LIMIT),
        name="mod",
    )(c_pad, ada_w, ada_b.reshape(DEPTH, 6, 1, D_MODEL))
    return out[:, :, :b].reshape(DEPTH, 6, b, 1, D_MODEL)


def _mod_spec(layer, idx):
    return pl.BlockSpec((None, None, None, 1, D_MODEL), lambda b, i: (layer, idx, b, 0, 0))


def _row_spec(width):
    return pl.BlockSpec((1, ROW_TILE, width), lambda b, i: (b, i, 0))


def _whole_spec(shape):
    return pl.BlockSpec(shape, lambda b, i: (0,) * len(shape))


def _vt_rows(maps_per_head):
    return HEAD_LANES + maps_per_head * SUM_ROWS


def _vt_shape(b, s, maps_per_head):
    return (b, DA_HEADS, s // ATTN_TILE, _vt_rows(maps_per_head), ATTN_TILE)


def _vt_out_spec(maps_per_head):
    return pl.BlockSpec((1, DA_HEADS, ROW_TILE // ATTN_TILE, _vt_rows(maps_per_head), ATTN_TILE),
                        lambda b, i: (b, 0, i, 0, 0))


def _store_vt(vt_ref, vt, maps_per_head):
    groups = DA_HEADS * maps_per_head
    vt = vt.astype(BF16).reshape(groups, HEAD_LANES // maps_per_head, ROW_TILE)
    ones = jnp.ones((groups, SUM_ROWS, ROW_TILE), BF16)
    vt = jnp.concatenate([vt, ones], axis=1).reshape(DA_HEADS, _vt_rows(maps_per_head), ROW_TILE)
    for t in range(ROW_TILE // ATTN_TILE):
        vt_ref[0, :, t] = vt[:, :, t * ATTN_TILE:(t + 1) * ATTN_TILE]


def _pre0_kernel(x_ref, sc_ref, sh_ref, wq_ref, wk_ref, wvt_ref, q_ref, k_ref, vt_ref, *, q_scale):
    u = (x_ref[0] * (1.0 + sc_ref[...]) + sh_ref[...]).astype(BF16)
    q_ref[0] = (_dot(u, wq_ref[...]) * q_scale).astype(BF16)
    k_ref[0] = _dot(u, wk_ref[...]).astype(BF16)
    _store_vt(vt_ref, _dot_nt(wvt_ref[...], u), 1)


def _pre0(x, mod, wq, wk, wvt):
    b, s, d = x.shape
    q_scale = DA_HEAD_DIM ** -0.5 * LOG2E
    return pl.pallas_call(
        functools.partial(_pre0_kernel, q_scale=q_scale),
        grid=(b, s // ROW_TILE),
        in_specs=[_row_spec(d), _mod_spec(0, 1), _mod_spec(0, 0),
                  _whole_spec((d, d)), _whole_spec((d, d)), _whole_spec((d, d))],
        out_specs=[_row_spec(d), _row_spec(d), _vt_out_spec(1)],
        out_shape=[jax.ShapeDtypeStruct((b, s, d), BF16),
                   jax.ShapeDtypeStruct((b, s, d), BF16),
                   jax.ShapeDtypeStruct(_vt_shape(b, s, 1), BF16)],
        compiler_params=pltpu.CompilerParams(vmem_limit_bytes=VMEM_LIMIT),
        name="pre0",
    )(x, mod, mod, wq, wk, wvt)


def _pre1_kernel(x_ref, sc_ref, sh_ref, wd_ref, qn_ref, kvn_ref, wuq_ref, wuk_ref, wuvt_ref,
                 cos_ref, sin_ref, q_ref, k_ref, vt_ref, *, q_scale):
    u = (x_ref[0] * (1.0 + sc_ref[...]) + sh_ref[...]).astype(BF16)
    down = _dot(u, wd_ref[...])
    c_q = _rms_norm_rows(down[:, :MLA_Q_LORA], qn_ref[...]).astype(BF16)
    c_kv = _rms_norm_rows(down[:, MLA_Q_LORA:MLA_Q_LORA + MLA_KV_LORA], kvn_ref[...]).astype(BF16)
    k_rope = down[:, MLA_Q_LORA + MLA_KV_LORA:]
    q = _dot(c_q, wuq_ref[...])
    k_nope = _dot(c_kv, wuk_ref[...])
    _store_vt(vt_ref, _dot_nt(wuvt_ref[...], c_kv), 2)

    cos = cos_ref[0]
    sin = sin_ref[0]
    lane = lax.broadcasted_iota(jnp.int32, cos.shape, 1)
    first_half = lane < MLA_NOPE + MLA_ROPE // 2

    def rope(xh):
        partner = jnp.where(first_half,
                            pltpu.roll(xh, HEAD_LANES - MLA_ROPE // 2, 1),
                            pltpu.roll(xh, MLA_ROPE // 2, 1))
        return xh * cos + partner * sin

    k_rope = rope(k_rope)
    for h in range(MLA_HEADS):
        sl = slice(h * HEAD_LANES, (h + 1) * HEAD_LANES)
        q_ref[0, :, sl] = (rope(q[:, sl]) * q_scale).astype(BF16)
        k_ref[0, :, sl] = (k_nope[:, sl] + k_rope).astype(BF16)


def _pre1(x, mod, wd, qn, kvn, wuq, wuk, wuvt, cos_t, sin_t):
    b, s, d = x.shape
    hw = MLA_HEADS * HEAD_LANES
    q_scale = (MLA_NOPE + MLA_ROPE) ** -0.5 * LOG2E
    return pl.pallas_call(
        functools.partial(_pre1_kernel, q_scale=q_scale),
        grid=(b, s // ROW_TILE),
        in_specs=[_row_spec(d), _mod_spec(1, 1), _mod_spec(1, 0),
                  _whole_spec(wd.shape), _whole_spec(qn.shape), _whole_spec(kvn.shape),
                  _whole_spec(wuq.shape), _whole_spec(wuk.shape), _whole_spec(wuvt.shape),
                  _row_spec(HEAD_LANES), _row_spec(HEAD_LANES)],
        out_specs=[_row_spec(hw), _row_spec(hw), _vt_out_spec(2)],
        out_shape=[jax.ShapeDtypeStruct((b, s, hw), BF16),
                   jax.ShapeDtypeStruct((b, s, hw), BF16),
                   jax.ShapeDtypeStruct(_vt_shape(b, s, 2), BF16)],
        compiler_params=pltpu.CompilerParams(vmem_limit_bytes=VMEM_LIMIT),
        name="pre1",
    )(x, mod, mod, wd, qn, kvn, wuq, wuk, wuvt, cos_t, sin_t)


def _t5_bucket(rel):
    nb = REL_BUCKETS // 2
    max_exact = nb // 2
    n_log = nb - max_exact
    thresholds = []
    for k in range(1, n_log):
        n = max_exact
        while n ** n_log * max_exact ** k < REL_MAX_DIST ** k * max_exact ** n_log:
            n += 1
        thresholds.append(n)
    n = np.abs(rel)
    large = max_exact + sum((n >= thr).astype(np.int32) for thr in thresholds)
    return (rel > 0).astype(np.int32) * nb + np.where(n < max_exact, n, large)


def _near_bucket_tiles():
    kk = np.arange(ATTN_TILE, dtype=np.int32)[:, None]
    qq = np.arange(ATTN_TILE, dtype=np.int32)[None, :]
    diag = np.where(kk // CHUNK <= qq // CHUNK, _t5_bucket(kk - qq), -1)
    prev = _t5_bucket(kk - ATTN_TILE - qq)
    return jnp.asarray(np.stack([diag, prev]).astype(np.int32))


def _bias_kernel(tab_ref, idx_ref, o_ref):
    h = pl.program_id(0)
    idx = idx_ref[...]
    val = jnp.zeros(idx.shape, F32)
    for bkt in range(REL_BUCKETS):
        val = jnp.where(idx == bkt, tab_ref[bkt, h], val)
    far = tab_ref[REL_BUCKETS // 2 - 1, h]
    o_ref[0] = jnp.where(idx < 0, NEG_BIG, (val - far) * LOG2E)


def _near_bias(rel_table):
    idx = _near_bucket_tiles()
    return pl.pallas_call(
        _bias_kernel,
        grid=(DA_HEADS,),
        in_specs=[pl.BlockSpec(memory_space=pltpu.SMEM),
                  pl.BlockSpec(idx.shape, lambda h: (0, 0, 0))],
        out_specs=pl.BlockSpec((1,) + idx.shape, lambda h: (h, 0, 0, 0)),
        out_shape=jax.ShapeDtypeStruct((DA_HEADS,) + idx.shape, F32),
        name="near_bias",
    )(rel_table, idx)


def _near_mask():
    kk = jnp.arange(ATTN_TILE, dtype=jnp.int32)[:, None]
    qq = jnp.arange(ATTN_TILE, dtype=jnp.int32)[None, :]
    return jnp.where(kk // CHUNK <= qq // CHUNK, 0.0, NEG_BIG).astype(F32)[None, None]


def _tile_tables(nq, n_near_kinds):
    ti, tj, ta = [], [], []
    for kind in range(n_near_kinds):
        for i in range(kind, nq):
            ti.append(i), tj.append(i - kind), ta.append(kind)
    n_near = len(ti)
    for j in range(nq - n_near_kinds):
        for i in range(j + n_near_kinds, nq):
            ti.append(i), tj.append(j), ta.append(0)
    return np.asarray(ti, np.int32), np.asarray(tj, np.int32), np.asarray(ta, np.int32), n_near


def _attn_kernel(*refs, diff, lam_init, n_near, n_tiles, tables):
    q_ref, k_ref, vt_ref, near_ref = refs[:4]
    refs = refs[4:]
    if diff:
        lam_ref, subw_ref = refs[:2]
        refs = refs[2:]
    o_ref, s_buf, p_buf, mx_buf, al_buf, m_scr, acc = refs[:7]
    t = ATTN_TILE
    unroll = s_buf.shape[0]
    nq = acc.shape[1]
    acc_rows = acc.shape[2]
    dv = acc_rows - SUM_ROWS

    if diff:
        qm = refs[7]
        q_all = q_ref[0]
        lane = lax.broadcasted_iota(jnp.int32, q_all.shape, 1)
        zero = jnp.zeros_like(q_all)
        qm[0] = jnp.where(lane < DA_HEAD_DIM, q_all, zero)
        qm[1] = jnp.where(lane >= DA_HEAD_DIM, q_all, zero)

        lam = (jnp.exp(jnp.sum(lam_ref[0:1] * lam_ref[1:2], axis=-1, keepdims=True))
               - jnp.exp(jnp.sum(lam_ref[2:3] * lam_ref[3:4], axis=-1, keepdims=True)) + lam_init)
        sub_w = subw_ref[...] * (1.0 - lam_init)

    ti_tab, tj_tab, ta_tab = ([int(v) for v in tab] for tab in tables)

    def rows(idx):
        return pl.ds(idx * t, t)

    def first_tile(n):
        return n < nq

    def q_map(mp, i):
        if diff:
            return qm[mp, rows(i), :]
        return q_ref[0, rows(i), mp * HEAD_LANES:(mp + 1) * HEAD_LANES]

    def score_dot(n, mp):
        i, j = ti_tab[n], tj_tab[n]
        k_m = k_ref[0, rows(j), :] if diff else k_ref[0, rows(j), mp * HEAD_LANES:(mp + 1) * HEAD_LANES]
        return _dot_nt(k_m, q_map(mp, i))

    def score_store(n, u, mp, s, near):
        s_buf[u, mp] = s
        if near:
            s = s_buf[u, mp] + near_ref[0, ta_tab[n]]
            s_buf[u, mp] = s
        mx_buf[u, mp] = jnp.broadcast_to(jnp.max(s, axis=0, keepdims=True), (SUBLANES, t))

    def per_query(x, stat):
        return x.reshape(x.shape[0] // SUBLANES, SUBLANES, t), stat[None]

    def exps(n, u, mp):
        i = ti_tab[n]
        if first_tile(n):
            m_new = mx_buf[u, mp]
        else:
            m_old = m_scr[mp, i]
            m_new = jnp.maximum(m_old, mx_buf[u, mp])
            al_buf[u, mp] = jnp.exp2(m_old - m_new)
        s3, m3 = per_query(s_buf[u, mp], m_new)
        p = jnp.exp2(s3 - m3).reshape(t, t)
        m_scr[mp, i] = m_new
        p_buf[u, mp] = p.astype(BF16)

    def pv_dot(n, u, mp):
        j = tj_tab[n]
        vt_m = vt_ref[0, 0, j] if diff else vt_ref[0, 0, j, mp * acc_rows:(mp + 1) * acc_rows, :]
        alpha = None if first_tile(n) else al_buf[u, mp]
        return _dot(vt_m, p_buf[u, mp]), alpha

    def acc_update(n, mp, pv, alpha):
        i = ti_tab[n]
        if first_tile(n):
            acc[mp, i] = pv
        else:
            a3, al3 = per_query(acc[mp, i], alpha)
            acc[mp, i] = (al3 * a3).reshape(acc_rows, t) + pv

    items = [(u, mp) for u in range(unroll) for mp in range(2)]

    def step(trip, near_flags, has1=True, has2=True, has3=True):
        pending = {}

        def issue(kk):
            u, mp = items[kk]
            pv = pv_dot(unroll * (trip - 2) + u, u, mp) if has3 else None
            s = score_dot(unroll * trip + u, mp) if has1 else None
            pending[kk] = (s, pv)

        def retire(kk):
            u, mp = items[kk]
            s, pv = pending.pop(kk)
            if has1:
                score_store(unroll * trip + u, u, mp, s, near_flags[u])
            if has3:
                acc_update(unroll * (trip - 2) + u, mp, *pv)

        for kk in range(min(DOT_LEAD, len(items))):
            issue(kk)
        for kk, (u, mp) in enumerate(items):
            if has2:
                exps(unroll * (trip - 1) + u, u, mp)
            if kk + DOT_LEAD < len(items):
                issue(kk + DOT_LEAD)
            if kk + DOT_LEAD - RETIRE_LAG >= 0:
                retire(kk + DOT_LEAD - RETIRE_LAG)
        for kk in sorted(pending):
            retire(kk)

    def finalize(i):
        o_a = acc[0, i, :dv] / acc[0, i, dv:dv + 1]
        o_b = acc[1, i, :dv] / acc[1, i, dv:dv + 1]
        if diff:
            o = o_a - lam * o_b
            o = o * lax.rsqrt(jnp.mean(o * o, axis=0, keepdims=True) + RMS_EPS) * sub_w
        else:
            o = jnp.concatenate([o_a, o_b], axis=0)
        o_ref[0, rows(i), :] = o.T.astype(BF16)

    n_trips = n_tiles // unroll
    last_step = {i: n // unroll + 2 for n, i in enumerate(ti_tab)}
    for trip in range(n_trips + 2):
        flags = [unroll * trip + u < n_near for u in range(unroll)]
        step(trip, flags, has1=trip < n_trips, has2=1 <= trip <= n_trips, has3=trip >= 2)
        for i in sorted(i for i, last in last_step.items() if last == trip):
            finalize(i)


def _attention(q, k, vt, near, lam_vecs=None, subln_w=None, *, diff, lam_init=0.0):
    b, s, _ = q.shape
    t = ATTN_TILE
    nq = s // t
    heads = DA_HEADS
    width = HEAD_LANES if diff else 2 * HEAD_LANES
    acc_rows = (HEAD_LANES if diff else MLA_V) + SUM_ROWS
    assert vt.shape[3] == (1 if diff else 2) * acc_rows
    near_heads = near.shape[0]
    ti, tj, ta, n_near = _tile_tables(nq, near.shape[1])
    n_tiles = len(ti)
    assert n_tiles % ATTN_UNROLL == 0 and n_tiles // ATTN_UNROLL >= 2
    in_specs = [
        pl.BlockSpec((1, s, width), lambda bi, h: (bi, 0, h)),
        pl.BlockSpec((1, s, width), lambda bi, h: (bi, 0, h)),
        pl.BlockSpec((1, 1, nq, vt.shape[3], t), lambda bi, h: (bi, h, 0, 0, 0)),
        pl.BlockSpec((1,) + near.shape[1:], lambda bi, h: (h if near_heads > 1 else 0, 0, 0, 0)),
    ]
    args = [q, k, vt, near]
    if diff:
        in_specs += [pl.BlockSpec(lam_vecs.shape, lambda bi, h: (0, 0)),
                     pl.BlockSpec(subln_w.shape, lambda bi, h: (0, 0))]
        args += [lam_vecs, subln_w]
    un = ATTN_UNROLL
    scratch = [pltpu.VMEM((un, 2, t, t), F32), pltpu.VMEM((un, 2, t, t), BF16),
               pltpu.VMEM((un, 2, SUBLANES, t), F32), pltpu.VMEM((un, 2, SUBLANES, t), F32),
               pltpu.VMEM((2, nq, SUBLANES, t), F32), pltpu.VMEM((2, nq, acc_rows, t), F32)]
    if diff:
        scratch.append(pltpu.VMEM((2, s, HEAD_LANES), BF16))
    return pl.pallas_call(
        functools.partial(_attn_kernel, diff=diff, lam_init=lam_init, n_near=n_near, n_tiles=n_tiles,
                          tables=(ti, tj, ta)),
        grid=(b, heads),
        in_specs=in_specs,
        out_specs=pl.BlockSpec((1, s, HEAD_LANES), lambda bi, h: (bi, 0, h)),
        out_shape=jax.ShapeDtypeStruct((b, s, heads * HEAD_LANES), BF16),
        scratch_shapes=scratch,
        compiler_params=pltpu.CompilerParams(vmem_limit_bytes=VMEM_LIMIT),
        name="attn_diff" if diff else "attn_mla",
    )(*args)


def _post_ffn_kernel(o_ref, x_ref, g1_ref, wo_ref, ln1g_ref, ln1b_ref, sc_ref, sh_ref, g2_ref,
                     w1_ref, w2_ref, ln2g_ref, ln2b_ref, out_ref):
    half = ROW_TILE // 2
    halves = (slice(0, half), slice(half, ROW_TILE))
    n_chunks = D_FF // FF_CHUNK

    def attn_out(r):
        return _dot(o_ref[0, r, :], wo_ref[...])

    def norm1(r, y):
        z = DEEPNORM_ALPHA * x_ref[0, r, :] + (1.0 + g1_ref[...]) * y
        x1 = _layer_norm_rows(z, ln1g_ref[...], ln1b_ref[...])
        return x1, (x1 * (1.0 + sc_ref[...]) + sh_ref[...]).astype(BF16)

    def mlp_chunk(u, y, c):
        sl = slice(c * FF_CHUNK, (c + 1) * FF_CHUNK)
        h = jnp.maximum(_dot(u, w1_ref[:, sl]), 0.0)
        upd = _dot((h * h).astype(BF16), w2_ref[sl, :])
        return upd if y is None else y + upd

    def norm2(r, x1, y):
        z = DEEPNORM_ALPHA * x1 + (1.0 + g2_ref[...]) * y
        out_ref[0, r, :] = _layer_norm_rows(z, ln2g_ref[...], ln2b_ref[...])

    ra, rb = halves
    ya, yb = attn_out(ra), attn_out(rb)
    x1a, ua = norm1(ra, ya)
    fa = mlp_chunk(ua, None, 0)
    x1b, ub = norm1(rb, yb)
    for c in range(1, n_chunks):
        fa = mlp_chunk(ua, fa, c)
    fb = mlp_chunk(ub, None, 0)
    norm2(ra, x1a, fa)
    for c in range(1, n_chunks):
        fb = mlp_chunk(ub, fb, c)
    norm2(rb, x1b, fb)


def _post_ffn(o, x, mod, layer, wo, w1, w2, ln_g, ln_b):
    b, s, d = x.shape
    resident = functools.partial(pl.BlockSpec, index_map=lambda bi, i: (0, 0), pipeline_mode=pl.Buffered(1))
    return pl.pallas_call(
        _post_ffn_kernel,
        grid=(b, s // ROW_TILE),
        in_specs=[_row_spec(d), _row_spec(d), _mod_spec(layer, 2), resident((d, d)),
                  _whole_spec((1, d)), _whole_spec((1, d)), _mod_spec(layer, 4), _mod_spec(layer, 3),
                  _mod_spec(layer, 5), resident((d, D_FF)), resident((D_FF, d)),
                  _whole_spec((1, d)), _whole_spec((1, d))],
        out_specs=_row_spec(d),
        out_shape=jax.ShapeDtypeStruct((b, s, d), F32),
        compiler_params=pltpu.CompilerParams(vmem_limit_bytes=VMEM_LIMIT),
        name=f"post_ffn{layer}",
    )(o, x, mod, wo, ln_g[0][None], ln_b[0][None], mod, mod, mod, w1, w2, ln_g[1][None], ln_b[1][None])


def _rope_tables(pos_offset, s):
    half = MLA_ROPE // 2
    pos = pos_offset[:, None] + jnp.arange(s, dtype=jnp.int32)[None, :]
    inv = ROPE_THETA ** (-jnp.arange(half, dtype=F32) / half)
    ang = pos.astype(F32)[..., None] * inv
    cos = jnp.cos(ang)
    sin = jnp.sin(ang)
    b = pos.shape[0]
    ones = jnp.ones((b, s, MLA_NOPE), F32)
    zeros_tail = jnp.zeros((b, s, HEAD_LANES - MLA_NOPE - MLA_ROPE), F32)
    cos_t = jnp.concatenate([ones, cos, cos, zeros_tail], axis=-1)
    sin_t = jnp.concatenate([jnp.zeros_like(ones), -sin, sin, zeros_tail], axis=-1)
    return cos_t, sin_t


def _pad_heads(w, per_head, lead):
    w = w.reshape(lead, MLA_HEADS, per_head)
    w = jnp.pad(w, ((0, 0), (0, 0), (0, HEAD_LANES - per_head)))
    return w.reshape(lead, MLA_HEADS * HEAD_LANES)


def kernel(x, c, pos_offset, ada_w, ada_b, ln_g, ln_b, rel_table, da_w_qkv, da_w_o, da_lam_q1, da_lam_k1, da_lam_q2, da_lam_k2, da_subln_w, mla_w_down, mla_q_norm_w, mla_w_uq, mla_kv_norm_w, mla_w_ukv, mla_w_o, ffn_w1, ffn_w2):
    b, s, d = x.shape
    assert d == D_MODEL and s % ROW_TILE == 0 and ROW_TILE % ATTN_TILE == 0 and ATTN_TILE % CHUNK == 0

    mod = _modulation(c, ada_w, ada_b)

    w_qkv = da_w_qkv[0].astype(BF16)
    q, k, vt = _pre0(x, mod, w_qkv[:, :d], w_qkv[:, d:2 * d], w_qkv[:, 2 * d:].T)
    lam_vecs = jnp.stack([da_lam_q1[0], da_lam_k1[0], da_lam_q2[0], da_lam_k2[0]])
    lam_init = 0.8 - 0.6 * math.exp(-0.3 * 0)
    o = _attention(q, k, vt, _near_bias(rel_table), lam_vecs, da_subln_w[0].reshape(-1, 1),
                   diff=True, lam_init=lam_init)
    x = _post_ffn(o, x, mod, 0, da_w_o[0].astype(BF16), ffn_w1[0].astype(BF16), ffn_w2[0].astype(BF16),
                  ln_g[0], ln_b[0])

    n_lat = MLA_Q_LORA + MLA_KV_LORA
    w_down = mla_w_down[0]
    w_down = jnp.concatenate(
        [w_down[:, :n_lat], jnp.zeros((d, MLA_NOPE), F32), w_down[:, n_lat:],
         jnp.zeros((d, HEAD_LANES - MLA_NOPE - MLA_ROPE), F32)], axis=-1).astype(BF16)
    w_uq = _pad_heads(mla_w_uq[0], MLA_NOPE + MLA_ROPE, MLA_Q_LORA).astype(BF16)
    w_ukv = mla_w_ukv[0].reshape(MLA_KV_LORA, MLA_HEADS, MLA_NOPE + MLA_V)
    w_uk = _pad_heads(w_ukv[..., :MLA_NOPE].reshape(MLA_KV_LORA, -1), MLA_NOPE, MLA_KV_LORA).astype(BF16)
    w_uvt = w_ukv[..., MLA_NOPE:].reshape(MLA_KV_LORA, -1).T.astype(BF16)
    cos_t, sin_t = _rope_tables(pos_offset, s)
    q, k, vt = _pre1(x, mod, w_down, mla_q_norm_w[0][None], mla_kv_norm_w[0][None],
                     w_uq, w_uk, w_uvt, cos_t, sin_t)
    o = _attention(q, k, vt, _near_mask(), diff=False)
    return _post_ffn(o, x, mod, 1, mla_w_o[0].astype(BF16), ffn_w1[1].astype(BF16), ffn_w2[1].astype(BF16),
                     ln_g[1], ln_b[1])
```

```python
import functools
import math

import jax
import jax.numpy as jnp
import numpy as np
from jax import lax
from jax.experimental import pallas as pl
from jax.experimental.pallas import tpu as pltpu

F32 = jnp.float32
BF16 = jnp.bfloat16

D_MODEL = 1024
DEPTH = 2
CHUNK = 64
DA_HEAD_DIM = 64
DA_HEADS = D_MODEL // (2 * DA_HEAD_DIM)
MLA_HEADS = 16
MLA_NOPE = 64
MLA_ROPE = 32
MLA_V = 64
MLA_Q_LORA = 512
MLA_KV_LORA = 256
ROPE_THETA = 10000.0
REL_BUCKETS = 32
REL_MAX_DIST = 128
D_FF = 4 * D_MODEL
DEEPNORM_ALPHA = (2 * DEPTH) ** 0.25
LN_EPS = 1e-5
RMS_EPS = 1e-6

LANES = 128
SUBLANES = 8
HEAD_LANES = 128
ATTN_TILE = 256
ATTN_UNROLL = 2
DOT_LEAD = 2
RETIRE_LAG = 3
SUM_ROWS = 16
ROW_TILE = 1024
FF_CHUNK = 1024
POST_GROUPS = 4
LOG2E = math.log2(math.e)
NEG_BIG = -1e30
VMEM_LIMIT = 56 * 1024 * 1024

_NT = (((1,), (1,)), ((), ()))


def _dot(a, b):
    return jnp.dot(a, b, preferred_element_type=F32)


def _dot_nt(a, b):
    return lax.dot_general(a, b, _NT, preferred_element_type=F32)


def _layer_norm_rows(z, g, b):
    mu = jnp.mean(z, axis=-1, keepdims=True)
    zc = z - mu
    var = jnp.mean(zc * zc, axis=-1, keepdims=True)
    return zc * lax.rsqrt(var + LN_EPS) * g + b


def _rms_norm_rows(z, w):
    return z * lax.rsqrt(jnp.mean(z * z, axis=-1, keepdims=True) + RMS_EPS) * w


def _mod_kernel(c_ref, w_ref, b_ref, o_ref):
    c = c_ref[...]
    c_act = c * jax.nn.sigmoid(c)
    o_ref[0, 0] = _dot(c_act.astype(BF16), w_ref[0].astype(BF16)) + b_ref[0, 0]


def _modulation(c, ada_w, ada_b):
    b = c.shape[0]
    rows = 8
    c_pad = jnp.zeros((rows, D_MODEL), F32).at[:b].set(c)
    out = pl.pallas_call(
        _mod_kernel,
        grid=(DEPTH, 6),
        in_specs=[
            pl.BlockSpec((rows, D_MODEL), lambda i, j: (0, 0)),
            pl.BlockSpec((1, D_MODEL, D_MODEL), lambda i, j: (i, 0, j)),
            pl.BlockSpec((1, 1, 1, D_MODEL), lambda i, j: (i, j, 0, 0)),
        ],
        out_specs=pl.BlockSpec((1, 1, rows, D_MODEL), lambda i, j: (i, j, 0, 0)),
        out_shape=jax.ShapeDtypeStruct((DEPTH, 6, rows, D_MODEL), F32),
        compiler_params=pltpu.CompilerParams(vmem_limit_bytes=VMEM_LIMIT),
        name="mod",
    )(c_pad, ada_w, ada_b.reshape(DEPTH, 6, 1, D_MODEL))
    return out[:, :, :b].reshape(DEPTH, 6, b, 1, D_MODEL)


def _mod_spec(layer, idx):
    return pl.BlockSpec((None, None, None, 1, D_MODEL), lambda b, i: (layer, idx, b, 0, 0))


def _row_spec(width):
    return pl.BlockSpec((1, ROW_TILE, width), lambda b, i: (b, i, 0))


def _whole_spec(shape):
    return pl.BlockSpec(shape, lambda b, i: (0,) * len(shape))


def _vt_rows(maps_per_head):
    return HEAD_LANES + maps_per_head * SUM_ROWS


def _vt_shape(b, s, maps_per_head):
    return (b, DA_HEADS, s // ATTN_TILE, _vt_rows(maps_per_head), ATTN_TILE)


def _vt_out_spec(maps_per_head):
    return pl.BlockSpec((1, DA_HEADS, ROW_TILE // ATTN_TILE, _vt_rows(maps_per_head), ATTN_TILE),
                        lambda b, i: (b, 0, i, 0, 0))


def _store_vt(vt_ref, vt, maps_per_head):
    groups = DA_HEADS * maps_per_head
    vt = vt.astype(BF16).reshape(groups, HEAD_LANES // maps_per_head, ROW_TILE)
    ones = jnp.ones((groups, SUM_ROWS, ROW_TILE), BF16)
    vt = jnp.concatenate([vt, ones], axis=1).reshape(DA_HEADS, _vt_rows(maps_per_head), ROW_TILE)
    for t in range(ROW_TILE // ATTN_TILE):
        vt_ref[0, :, t] = vt[:, :, t * ATTN_TILE:(t + 1) * ATTN_TILE]


def _pre0_kernel(x_ref, sc_ref, sh_ref, wq_ref, wk_ref, wvt_ref, q_ref, k_ref, vt_ref, *, q_scale):
    u = (x_ref[0] * (1.0 + sc_ref[...]) + sh_ref[...]).astype(BF16)
    q_ref[0] = (_dot(u, wq_ref[...]) * q_scale).astype(BF16)
    k_ref[0] = _dot(u, wk_ref[...]).astype(BF16)
    _store_vt(vt_ref, _dot_nt(wvt_ref[...], u), 1)


def _pre0(x, mod, wq, wk, wvt):
    b, s, d = x.shape
    q_scale = DA_HEAD_DIM ** -0.5 * LOG2E
    return pl.pallas_call(
        functools.partial(_pre0_kernel, q_scale=q_scale),
        grid=(b, s // ROW_TILE),
        in_specs=[_row_spec(d), _mod_spec(0, 1), _mod_spec(0, 0),
                  _whole_spec((d, d)), _whole_spec((d, d)), _whole_spec((d, d))],
        out_specs=[_row_spec(d), _row_spec(d), _vt_out_spec(1)],
        out_shape=[jax.ShapeDtypeStruct((b, s, d), BF16),
                   jax.ShapeDtypeStruct((b, s, d), BF16),
                   jax.ShapeDtypeStruct(_vt_shape(b, s, 1), BF16)],
        compiler_params=pltpu.CompilerParams(vmem_limit_bytes=VMEM_LIMIT),
        name="pre0",
    )(x, mod, mod, wq, wk, wvt)


def _pre1_kernel(x_ref, sc_ref, sh_ref, wd_ref, qn_ref, kvn_ref, wuq_ref, wuk_ref, wuvt_ref,
                 cos_ref, sin_ref, q_ref, k_ref, vt_ref, *, q_scale):
    u = (x_ref[0] * (1.0 + sc_ref[...]) + sh_ref[...]).astype(BF16)
    down = _dot(u, wd_ref[...])
    c_q = _rms_norm_rows(down[:, :MLA_Q_LORA], qn_ref[...]).astype(BF16)
    c_kv = _rms_norm_rows(down[:, MLA_Q_LORA:MLA_Q_LORA + MLA_KV_LORA], kvn_ref[...]).astype(BF16)
    k_rope = down[:, MLA_Q_LORA + MLA_KV_LORA:]
    q = _dot(c_q, wuq_ref[...])
    k_nope = _dot(c_kv, wuk_ref[...])
    _store_vt(vt_ref, _dot_nt(wuvt_ref[...], c_kv), 2)

    cos = cos_ref[0]
    sin = sin_ref[0]
    lane = lax.broadcasted_iota(jnp.int32, cos.shape, 1)
    first_half = lane < MLA_NOPE + MLA_ROPE // 2

    def rope(xh):
        partner = jnp.where(first_half,
                            pltpu.roll(xh, HEAD_LANES - MLA_ROPE // 2, 1),
                            pltpu.roll(xh, MLA_ROPE // 2, 1))
        return xh * cos + partner * sin

    k_rope = rope(k_rope)
    for h in range(MLA_HEADS):
        sl = slice(h * HEAD_LANES, (h + 1) * HEAD_LANES)
        q_ref[0, :, sl] = (rope(q[:, sl]) * q_scale).astype(BF16)
        k_ref[0, :, sl] = (k_nope[:, sl] + k_rope).astype(BF16)


def _pre1(x, mod, wd, qn, kvn, wuq, wuk, wuvt, cos_t, sin_t):
    b, s, d = x.shape
    hw = MLA_HEADS * HEAD_LANES
    q_scale = (MLA_NOPE + MLA_ROPE) ** -0.5 * LOG2E
    return pl.pallas_call(
        functools.partial(_pre1_kernel, q_scale=q_scale),
        grid=(b, s // ROW_TILE),
        in_specs=[_row_spec(d), _mod_spec(1, 1), _mod_spec(1, 0),
                  _whole_spec(wd.shape), _whole_spec(qn.shape), _whole_spec(kvn.shape),
                  _whole_spec(wuq.shape), _whole_spec(wuk.shape), _whole_spec(wuvt.shape),
                  _row_spec(HEAD_LANES), _row_spec(HEAD_LANES)],
        out_specs=[_row_spec(hw), _row_spec(hw), _vt_out_spec(2)],
        out_shape=[jax.ShapeDtypeStruct((b, s, hw), BF16),
                   jax.ShapeDtypeStruct((b, s, hw), BF16),
                   jax.ShapeDtypeStruct(_vt_shape(b, s, 2), BF16)],
        compiler_params=pltpu.CompilerParams(vmem_limit_bytes=VMEM_LIMIT),
        name="pre1",
    )(x, mod, mod, wd, qn, kvn, wuq, wuk, wuvt, cos_t, sin_t)


def _t5_bucket(rel):
    nb = REL_BUCKETS // 2
    max_exact = nb // 2
    n_log = nb - max_exact
    thresholds = []
    for k in range(1, n_log):
        n = max_exact
        while n ** n_log * max_exact ** k < REL_MAX_DIST ** k * max_exact ** n_log:
            n += 1
        thresholds.append(n)
    n = np.abs(rel)
    large = max_exact + sum((n >= thr).astype(np.int32) for thr in thresholds)
    return (rel > 0).astype(np.int32) * nb + np.where(n < max_exact, n, large)


def _near_bucket_tiles():
    kk = np.arange(ATTN_TILE, dtype=np.int32)[:, None]
    qq = np.arange(ATTN_TILE, dtype=np.int32)[None, :]
    diag = np.where(kk // CHUNK <= qq // CHUNK, _t5_bucket(kk - qq), -1)
    prev = _t5_bucket(kk - ATTN_TILE - qq)
    return jnp.asarray(np.stack([diag, prev]).astype(np.int32))


def _bias_kernel(tab_ref, idx_ref, o_ref):
    h = pl.program_id(0)
    idx = idx_ref[...]
    val = jnp.zeros(idx.shape, F32)
    for bkt in range(REL_BUCKETS):
        val = jnp.where(idx == bkt, tab_ref[bkt, h], val)
    far = tab_ref[REL_BUCKETS // 2 - 1, h]
    o_ref[0] = jnp.where(idx < 0, NEG_BIG, (val - far) * LOG2E)


def _near_bias(rel_table):
    idx = _near_bucket_tiles()
    return pl.pallas_call(
        _bias_kernel,
        grid=(DA_HEADS,),
        in_specs=[pl.BlockSpec(memory_space=pltpu.SMEM),
                  pl.BlockSpec(idx.shape, lambda h: (0, 0, 0))],
        out_specs=pl.BlockSpec((1,) + idx.shape, lambda h: (h, 0, 0, 0)),
        out_shape=jax.ShapeDtypeStruct((DA_HEADS,) + idx.shape, F32),
        name="near_bias",
    )(rel_table, idx)


def _near_mask():
    kk = jnp.arange(ATTN_TILE, dtype=jnp.int32)[:, None]
    qq = jnp.arange(ATTN_TILE, dtype=jnp.int32)[None, :]
    return jnp.where(kk // CHUNK <= qq // CHUNK, 0.0, NEG_BIG).astype(F32)[None, None]


def _tile_tables(nq, n_near_kinds):
    ti, tj, ta = [], [], []
    for kind in range(n_near_kinds):
        for i in range(kind, nq):
            ti.append(i), tj.append(i - kind), ta.append(kind)
    n_near = len(ti)
    for j in range(nq - n_near_kinds):
        for i in range(j + n_near_kinds, nq):
            ti.append(i), tj.append(j), ta.append(0)
    return np.asarray(ti, np.int32), np.asarray(tj, np.int32), np.asarray(ta, np.int32), n_near


def _attn_kernel(*refs, diff, lam_init, n_near, n_tiles, tables):
    q_ref, k_ref, vt_ref, near_ref = refs[:4]
    refs = refs[4:]
    if diff:
        lam_ref, subw_ref = refs[:2]
        refs = refs[2:]
    o_ref, s_buf, p_buf, mx_buf, al_buf, m_scr, acc = refs[:7]
    t = ATTN_TILE
    unroll = s_buf.shape[0]
    nq = acc.shape[1]
    acc_rows = acc.shape[2]
    dv = acc_rows - SUM_ROWS

    if diff:
        qm = refs[7]
        q_all = q_ref[0]
        lane = lax.broadcasted_iota(jnp.int32, q_all.shape, 1)
        zero = jnp.zeros_like(q_all)
        qm[0] = jnp.where(lane < DA_HEAD_DIM, q_all, zero)
        qm[1] = jnp.where(lane >= DA_HEAD_DIM, q_all, zero)

        lam = (jnp.exp(jnp.sum(lam_ref[0:1] * lam_ref[1:2], axis=-1, keepdims=True))
               - jnp.exp(jnp.sum(lam_ref[2:3] * lam_ref[3:4], axis=-1, keepdims=True)) + lam_init)
        sub_w = subw_ref[...] * (1.0 - lam_init)

    ti_tab, tj_tab, ta_tab = ([int(v) for v in tab] for tab in tables)

    def rows(idx):
        return pl.ds(idx * t, t)

    def first_tile(n):
        return n < nq

    def q_map(mp, i):
        if diff:
            return qm[mp, rows(i), :]
        return q_ref[0, rows(i), mp * HEAD_LANES:(mp + 1) * HEAD_LANES]

    def score_dot(n, mp):
        i, j = ti_tab[n], tj_tab[n]
        k_m = k_ref[0, rows(j), :] if diff else k_ref[0, rows(j), mp * HEAD_LANES:(mp + 1) * HEAD_LANES]
        return _dot_nt(k_m, q_map(mp, i))

    def score_store(n, u, mp, s, near):
        s_buf[u, mp] = s
        if near:
            s = s_buf[u, mp] + near_ref[0, ta_tab[n]]
            s_buf[u, mp] = s
        mx_buf[u, mp] = jnp.broadcast_to(jnp.max(s, axis=0, keepdims=True), (SUBLANES, t))

    def per_query(x, stat):
        return x.reshape(x.shape[0] // SUBLANES, SUBLANES, t), stat[None]

    def exps(n, u, mp):
        i = ti_tab[n]
        if first_tile(n):
            m_new = mx_buf[u, mp]
        else:
            m_old = m_scr[mp, i]
            m_new = jnp.maximum(m_old, mx_buf[u, mp])
            al_buf[u, mp] = jnp.exp2(m_old - m_new)
        s3, m3 = per_query(s_buf[u, mp], m_new)
        p = jnp.exp2(s3 - m3).reshape(t, t)
        m_scr[mp, i] = m_new
        p_buf[u, mp] = p.astype(BF16)

    def pv_dot(n, u, mp):
        j = tj_tab[n]
        vt_m = vt_ref[0, 0, j] if diff else vt_ref[0, 0, j, mp * acc_rows:(mp + 1) * acc_rows, :]
        alpha = None if first_tile(n) else al_buf[u, mp]
        return _dot(vt_m, p_buf[u, mp]), alpha

    def acc_update(n, mp, pv, alpha):
        i = ti_tab[n]
        if first_tile(n):
            acc[mp, i] = pv
        else:
            a3, al3 = per_query(acc[mp, i], alpha)
            acc[mp, i] = (al3 * a3).reshape(acc_rows, t) + pv

    items = [(u, mp) for u in range(unroll) for mp in range(2)]

    def step(trip, near_flags, has1=True, has2=True, has3=True):
        pending = {}

        def issue(kk):
            u, mp = items[kk]
            pv = pv_dot(unroll * (trip - 2) + u, u, mp) if has3 else None
            s = score_dot(unroll * trip + u, mp) if has1 else None
            pending[kk] = (s, pv)

        def retire(kk):
            u, mp = items[kk]
            s, pv = pending.pop(kk)
            if has1:
                score_store(unroll * trip + u, u, mp, s, near_flags[u])
            if has3:
                acc_update(unroll * (trip - 2) + u, mp, *pv)

        for kk in range(min(DOT_LEAD, len(items))):
            issue(kk)
        for kk, (u, mp) in enumerate(items):
            if has2:
                exps(unroll * (trip - 1) + u, u, mp)
            if kk + DOT_LEAD < len(items):
                issue(kk + DOT_LEAD)
            if kk + DOT_LEAD - RETIRE_LAG >= 0:
                retire(kk + DOT_LEAD - RETIRE_LAG)
        for kk in sorted(pending):
            retire(kk)

    def finalize(i):
        o_a = acc[0, i, :dv] / acc[0, i, dv:dv + 1]
        o_b = acc[1, i, :dv] / acc[1, i, dv:dv + 1]
        if diff:
            o = o_a - lam * o_b
            o = o * lax.rsqrt(jnp.mean(o * o, axis=0, keepdims=True) + RMS_EPS) * sub_w
        else:
            o = jnp.concatenate([o_a, o_b], axis=0)
        o_ref[0, rows(i), :] = o.T.astype(BF16)

    n_trips = n_tiles // unroll
    last_step = {i: n // unroll + 2 for n, i in enumerate(ti_tab)}
    for trip in range(n_trips + 2):
        flags = [unroll * trip + u < n_near for u in range(unroll)]
        step(trip, flags, has1=trip < n_trips, has2=1 <= trip <= n_trips, has3=trip >= 2)
        for i in sorted(i for i, last in last_step.items() if last == trip):
            finalize(i)


def _attention(q, k, vt, near, lam_vecs=None, subln_w=None, *, diff, lam_init=0.0):
    b, s, _ = q.shape
    t = ATTN_TILE
    nq = s // t
    heads = DA_HEADS
    width = HEAD_LANES if diff else 2 * HEAD_LANES
    acc_rows = (HEAD_LANES if diff else MLA_V) + SUM_ROWS
    assert vt.shape[3] == (1 if diff else 2) * acc_rows
    near_heads = near.shape[0]
    ti, tj, ta, n_near = _tile_tables(nq, near.shape[1])
    n_tiles = len(ti)
    assert n_tiles % ATTN_UNROLL == 0 and n_tiles // ATTN_UNROLL >= 2
    in_specs = [
        pl.BlockSpec((1, s, width), lambda bi, h: (bi, 0, h)),
        pl.BlockSpec((1, s, width), lambda bi, h: (bi, 0, h)),
        pl.BlockSpec((1, 1, nq, vt.shape[3], t), lambda bi, h: (bi, h, 0, 0, 0)),
        pl.BlockSpec((1,) + near.shape[1:], lambda bi, h: (h if near_heads > 1 else 0, 0, 0, 0)),
    ]
    args = [q, k, vt, near]
    if diff:
        in_specs += [pl.BlockSpec(lam_vecs.shape, lambda bi, h: (0, 0)),
                     pl.BlockSpec(subln_w.shape, lambda bi, h: (0, 0))]
        args += [lam_vecs, subln_w]
    un = ATTN_UNROLL
    scratch = [pltpu.VMEM((un, 2, t, t), F32), pltpu.VMEM((un, 2, t, t), BF16),
               pltpu.VMEM((un, 2, SUBLANES, t), F32), pltpu.VMEM((un, 2, SUBLANES, t), F32),
               pltpu.VMEM((2, nq, SUBLANES, t), F32), pltpu.VMEM((2, nq, acc_rows, t), F32)]
    if diff:
        scratch.append(pltpu.VMEM((2, s, HEAD_LANES), BF16))
    return pl.pallas_call(
        functools.partial(_attn_kernel, diff=diff, lam_init=lam_init, n_near=n_near, n_tiles=n_tiles,
                          tables=(ti, tj, ta)),
        grid=(b, heads),
        in_specs=in_specs,
        out_specs=pl.BlockSpec((1, s, HEAD_LANES), lambda bi, h: (bi, 0, h)),
        out_shape=jax.ShapeDtypeStruct((b, s, heads * HEAD_LANES), BF16),
        scratch_shapes=scratch,
        compiler_params=pltpu.CompilerParams(vmem_limit_bytes=VMEM_LIMIT),
        name="attn_diff" if diff else "attn_mla",
    )(*args)


def _post_ffn_kernel(o_ref, x_ref, g1_ref, wo_ref, ln1g_ref, ln1b_ref, sc_ref, sh_ref, g2_ref,
                     w1_ref, w2_ref, ln2g_ref, ln2b_ref, out_ref):
    group = ROW_TILE // POST_GROUPS
    groups = [slice(g * group, (g + 1) * group) for g in range(POST_GROUPS)]
    n_chunks = D_FF // FF_CHUNK

    def attn_out(r):
        return _dot(o_ref[0, r, :], wo_ref[...])

    def norm1(r, y):
        z = DEEPNORM_ALPHA * x_ref[0, r, :] + (1.0 + g1_ref[...]) * y
        x1 = _layer_norm_rows(z, ln1g_ref[...], ln1b_ref[...])
        return x1, (x1 * (1.0 + sc_ref[...]) + sh_ref[...]).astype(BF16)

    def mlp_chunk(u, y, c):
        sl = slice(c * FF_CHUNK, (c + 1) * FF_CHUNK)
        h = jnp.maximum(_dot(u, w1_ref[:, sl]), 0.0)
        upd = _dot((h * h).astype(BF16), w2_ref[sl, :])
        return upd if y is None else y + upd

    def norm2(r, x1, y):
        z = DEEPNORM_ALPHA * x1 + (1.0 + g2_ref[...]) * y
        out_ref[0, r, :] = _layer_norm_rows(z, ln2g_ref[...], ln2b_ref[...])

    ys = [attn_out(r) for r in groups]
    normed = {0: norm1(groups[0], ys[0])}
    mlp_out = {}
    for g, r in enumerate(groups):
        x1, u = normed[g]
        f = mlp_chunk(u, None, 0)
        if g + 1 < POST_GROUPS:
            normed[g + 1] = norm1(groups[g + 1], ys[g + 1])
        if g >= 1:
            norm2(groups[g - 1], normed[g - 1][0], mlp_out[g - 1])
        for c in range(1, n_chunks):
            f = mlp_chunk(u, f, c)
        mlp_out[g] = f
    last = POST_GROUPS - 1
    norm2(groups[last], normed[last][0], mlp_out[last])


def _post_ffn(o, x, mod, layer, wo, w1, w2, ln_g, ln_b):
    b, s, d = x.shape
    resident = functools.partial(pl.BlockSpec, index_map=lambda bi, i: (0, 0), pipeline_mode=pl.Buffered(1))
    return pl.pallas_call(
        _post_ffn_kernel,
        grid=(b, s // ROW_TILE),
        in_specs=[_row_spec(d), _row_spec(d), _mod_spec(layer, 2), resident((d, d)),
                  _whole_spec((1, d)), _whole_spec((1, d)), _mod_spec(layer, 4), _mod_spec(layer, 3),
                  _mod_spec(layer, 5), resident((d, D_FF)), resident((D_FF, d)),
                  _whole_spec((1, d)), _whole_spec((1, d))],
        out_specs=_row_spec(d),
        out_shape=jax.ShapeDtypeStruct((b, s, d), F32),
        compiler_params=pltpu.CompilerParams(vmem_limit_bytes=VMEM_LIMIT),
        name=f"post_ffn{layer}",
    )(o, x, mod, wo, ln_g[0][None], ln_b[0][None], mod, mod, mod, w1, w2, ln_g[1][None], ln_b[1][None])


def _rope_tables(pos_offset, s):
    half = MLA_ROPE // 2
    pos = pos_offset[:, None] + jnp.arange(s, dtype=jnp.int32)[None, :]
    inv = ROPE_THETA ** (-jnp.arange(half, dtype=F32) / half)
    ang = pos.astype(F32)[..., None] * inv
    cos = jnp.cos(ang)
    sin = jnp.sin(ang)
    b = pos.shape[0]
    ones = jnp.ones((b, s, MLA_NOPE), F32)
    zeros_tail = jnp.zeros((b, s, HEAD_LANES - MLA_NOPE - MLA_ROPE), F32)
    cos_t = jnp.concatenate([ones, cos, cos, zeros_tail], axis=-1)
    sin_t = jnp.concatenate([jnp.zeros_like(ones), -sin, sin, zeros_tail], axis=-1)
    return cos_t, sin_t


def _pad_heads(w, per_head, lead):
    w = w.reshape(lead, MLA_HEADS, per_head)
    w = jnp.pad(w, ((0, 0), (0, 0), (0, HEAD_LANES - per_head)))
    return w.reshape(lead, MLA_HEADS * HEAD_LANES)


def kernel(x, c, pos_offset, ada_w, ada_b, ln_g, ln_b, rel_table, da_w_qkv, da_w_o, da_lam_q1, da_lam_k1, da_lam_q2, da_lam_k2, da_subln_w, mla_w_down, mla_q_norm_w, mla_w_uq, mla_kv_norm_w, mla_w_ukv, mla_w_o, ffn_w1, ffn_w2):
    b, s, d = x.shape
    assert d == D_MODEL and s % ROW_TILE == 0 and ROW_TILE % ATTN_TILE == 0 and ATTN_TILE % CHUNK == 0

    mod = _modulation(c, ada_w, ada_b)

    w_qkv = da_w_qkv[0].astype(BF16)
    q, k, vt = _pre0(x, mod, w_qkv[:, :d], w_qkv[:, d:2 * d], w_qkv[:, 2 * d:].T)
    lam_vecs = jnp.stack([da_lam_q1[0], da_lam_k1[0], da_lam_q2[0], da_lam_k2[0]])
    lam_init = 0.8 - 0.6 * math.exp(-0.3 * 0)
    o = _attention(q, k, vt, _near_bias(rel_table), lam_vecs, da_subln_w[0].reshape(-1, 1),
                   diff=True, lam_init=lam_init)
    x = _post_ffn(o, x, mod, 0, da_w_o[0].astype(BF16), ffn_w1[0].astype(BF16), ffn_w2[0].astype(BF16),
                  ln_g[0], ln_b[0])

    n_lat = MLA_Q_LORA + MLA_KV_LORA
    w_down = mla_w_down[0]
    w_down = jnp.concatenate(
        [w_down[:, :n_lat], jnp.zeros((d, MLA_NOPE), F32), w_down[:, n_lat:],
         jnp.zeros((d, HEAD_LANES - MLA_NOPE - MLA_ROPE), F32)], axis=-1).astype(BF16)
    w_uq = _pad_heads(mla_w_uq[0], MLA_NOPE + MLA_ROPE, MLA_Q_LORA).astype(BF16)
    w_ukv = mla_w_ukv[0].reshape(MLA_KV_LORA, MLA_HEADS, MLA_NOPE + MLA_V)
    w_uk = _pad_heads(w_ukv[..., :MLA_NOPE].reshape(MLA_KV_LORA, -1), MLA_NOPE, MLA_KV_LORA).astype(BF16)
    w_uvt = w_ukv[..., MLA_NOPE:].reshape(MLA_KV_LORA, -1).T.astype(BF16)
    cos_t, sin_t = _rope_tables(pos_offset, s)
    q, k, vt = _pre1(x, mod, w_down, mla_q_norm_w[0][None], mla_kv_norm_w[0][None],
                     w_uq, w_uk, w_uvt, cos_t, sin_t)
    o = _attention(q, k, vt, _near_mask(), diff=False)
    return _post_ffn(o, x, mod, 1, mla_w_o[0].astype(BF16), ffn_w1[1].astype(BF16), ffn_w2[1].astype(BF16),
                     ln_g[1], ln_b[1])
```

```python
import functools
import math

import jax
import jax.numpy as jnp
import numpy as np
from jax import lax
from jax.experimental import pallas as pl
from jax.experimental.pallas import tpu as pltpu

F32 = jnp.float32
BF16 = jnp.bfloat16

D_MODEL = 1024
DEPTH = 2
CHUNK = 64
DA_HEAD_DIM = 64
DA_HEADS = D_MODEL // (2 * DA_HEAD_DIM)
MLA_HEADS = 16
MLA_NOPE = 64
MLA_ROPE = 32
MLA_V = 64
MLA_Q_LORA = 512
MLA_KV_LORA = 256
ROPE_THETA = 10000.0
REL_BUCKETS = 32
REL_MAX_DIST = 128
D_FF = 4 * D_MODEL
DEEPNORM_ALPHA = (2 * DEPTH) ** 0.25
LN_EPS = 1e-5
RMS_EPS = 1e-6

LANES = 128
SUBLANES = 8
HEAD_LANES = 128
ATTN_TILE = 256
ATTN_UNROLL = 2
DOT_LEAD = 3
RETIRE_LAG = 4
SUM_ROWS = 16
ROW_TILE = 1024
FF_CHUNK = 1024
LOG2E = math.log2(math.e)
NEG_BIG = -1e30
VMEM_LIMIT = 56 * 1024 * 1024

_NT = (((1,), (1,)), ((), ()))


def _dot(a, b):
    return jnp.dot(a, b, preferred_element_type=F32)


def _dot_nt(a, b):
    return lax.dot_general(a, b, _NT, preferred_element_type=F32)


def _layer_norm_rows(z, g, b):
    mu = jnp.mean(z, axis=-1, keepdims=True)
    zc = z - mu
    var = jnp.mean(zc * zc, axis=-1, keepdims=True)
    return zc * lax.rsqrt(var + LN_EPS) * g + b


def _rms_norm_rows(z, w):
    return z * lax.rsqrt(jnp.mean(z * z, axis=-1, keepdims=True) + RMS_EPS) * w


def _mod_kernel(c_ref, w_ref, b_ref, o_ref):
    c = c_ref[...]
    c_act = c * jax.nn.sigmoid(c)
    o_ref[0, 0] = _dot(c_act.astype(BF16), w_ref[0].astype(BF16)) + b_ref[0, 0]


def _modulation(c, ada_w, ada_b):
    b = c.shape[0]
    rows = 8
    c_pad = jnp.zeros((rows, D_MODEL), F32).at[:b].set(c)
    out = pl.pallas_call(
        _mod_kernel,
        grid=(DEPTH, 6),
        in_specs=[
            pl.BlockSpec((rows, D_MODEL), lambda i, j: (0, 0)),
            pl.BlockSpec((1, D_MODEL, D_MODEL), lambda i, j: (i, 0, j)),
            pl.BlockSpec((1, 1, 1, D_MODEL), lambda i, j: (i, j, 0, 0)),
        ],
        out_specs=pl.BlockSpec((1, 1, rows, D_MODEL), lambda i, j: (i, j, 0, 0)),
        out_shape=jax.ShapeDtypeStruct((DEPTH, 6, rows, D_MODEL), F32),
        compiler_params=pltpu.CompilerParams(vmem_limit_bytes=VMEM_LIMIT),
        name="mod",
    )(c_pad, ada_w, ada_b.reshape(DEPTH, 6, 1, D_MODEL))
    return out[:, :, :b].reshape(DEPTH, 6, b, 1, D_MODEL)


def _mod_spec(layer, idx):
    return pl.BlockSpec((None, None, None, 1, D_MODEL), lambda b, i: (layer, idx, b, 0, 0))


def _row_spec(width):
    return pl.BlockSpec((1, ROW_TILE, width), lambda b, i: (b, i, 0))


def _whole_spec(shape):
    return pl.BlockSpec(shape, lambda b, i: (0,) * len(shape))


def _vt_rows(maps_per_head):
    return HEAD_LANES + maps_per_head * SUM_ROWS


def _vt_shape(b, s, maps_per_head):
    return (b, DA_HEADS, s // ATTN_TILE, _vt_rows(maps_per_head), ATTN_TILE)


def _vt_out_spec(maps_per_head):
    return pl.BlockSpec((1, DA_HEADS, ROW_TILE // ATTN_TILE, _vt_rows(maps_per_head), ATTN_TILE),
                        lambda b, i: (b, 0, i, 0, 0))


def _store_vt(vt_ref, vt, maps_per_head):
    groups = DA_HEADS * maps_per_head
    vt = vt.astype(BF16).reshape(groups, HEAD_LANES // maps_per_head, ROW_TILE)
    ones = jnp.ones((groups, SUM_ROWS, ROW_TILE), BF16)
    vt = jnp.concatenate([vt, ones], axis=1).reshape(DA_HEADS, _vt_rows(maps_per_head), ROW_TILE)
    for t in range(ROW_TILE // ATTN_TILE):
        vt_ref[0, :, t] = vt[:, :, t * ATTN_TILE:(t + 1) * ATTN_TILE]


def _pre0_kernel(x_ref, sc_ref, sh_ref, wq_ref, wk_ref, wvt_ref, q_ref, k_ref, vt_ref, *, q_scale):
    u = (x_ref[0] * (1.0 + sc_ref[...]) + sh_ref[...]).astype(BF16)
    q_ref[0] = (_dot(u, wq_ref[...]) * q_scale).astype(BF16)
    k_ref[0] = _dot(u, wk_ref[...]).astype(BF16)
    _store_vt(vt_ref, _dot_nt(wvt_ref[...], u), 1)


def _pre0(x, mod, wq, wk, wvt):
    b, s, d = x.shape
    q_scale = DA_HEAD_DIM ** -0.5 * LOG2E
    return pl.pallas_call(
        functools.partial(_pre0_kernel, q_scale=q_scale),
        grid=(b, s // ROW_TILE),
        in_specs=[_row_spec(d), _mod_spec(0, 1), _mod_spec(0, 0),
                  _whole_spec((d, d)), _whole_spec((d, d)), _whole_spec((d, d))],
        out_specs=[_row_spec(d), _row_spec(d), _vt_out_spec(1)],
        out_shape=[jax.ShapeDtypeStruct((b, s, d), BF16),
                   jax.ShapeDtypeStruct((b, s, d), BF16),
                   jax.ShapeDtypeStruct(_vt_shape(b, s, 1), BF16)],
        compiler_params=pltpu.CompilerParams(vmem_limit_bytes=VMEM_LIMIT),
        name="pre0",
    )(x, mod, mod, wq, wk, wvt)


def _pre1_kernel(x_ref, sc_ref, sh_ref, wd_ref, qn_ref, kvn_ref, wuq_ref, wuk_ref, wuvt_ref,
                 cos_ref, sin_ref, q_ref, k_ref, vt_ref, *, q_scale):
    u = (x_ref[0] * (1.0 + sc_ref[...]) + sh_ref[...]).astype(BF16)
    down = _dot(u, wd_ref[...])
    c_q = _rms_norm_rows(down[:, :MLA_Q_LORA], qn_ref[...]).astype(BF16)
    c_kv = _rms_norm_rows(down[:, MLA_Q_LORA:MLA_Q_LORA + MLA_KV_LORA], kvn_ref[...]).astype(BF16)
    k_rope = down[:, MLA_Q_LORA + MLA_KV_LORA:]
    q = _dot(c_q, wuq_ref[...])
    k_nope = _dot(c_kv, wuk_ref[...])
    _store_vt(vt_ref, _dot_nt(wuvt_ref[...], c_kv), 2)

    cos = cos_ref[0]
    sin = sin_ref[0]
    lane = lax.broadcasted_iota(jnp.int32, cos.shape, 1)
    first_half = lane < MLA_NOPE + MLA_ROPE // 2

    def rope(xh):
        partner = jnp.where(first_half,
                            pltpu.roll(xh, HEAD_LANES - MLA_ROPE // 2, 1),
                            pltpu.roll(xh, MLA_ROPE // 2, 1))
        return xh * cos + partner * sin

    k_rope = rope(k_rope)
    for h in range(MLA_HEADS):
        sl = slice(h * HEAD_LANES, (h + 1) * HEAD_LANES)
        q_ref[0, :, sl] = (rope(q[:, sl]) * q_scale).astype(BF16)
        k_ref[0, :, sl] = (k_nope[:, sl] + k_rope).astype(BF16)


def _pre1(x, mod, wd, qn, kvn, wuq, wuk, wuvt, cos_t, sin_t):
    b, s, d = x.shape
    hw = MLA_HEADS * HEAD_LANES
    q_scale = (MLA_NOPE + MLA_ROPE) ** -0.5 * LOG2E
    return pl.pallas_call(
        functools.partial(_pre1_kernel, q_scale=q_scale),
        grid=(b, s // ROW_TILE),
        in_specs=[_row_spec(d), _mod_spec(1, 1), _mod_spec(1, 0),
                  _whole_spec(wd.shape), _whole_spec(qn.shape), _whole_spec(kvn.shape),
                  _whole_spec(wuq.shape), _whole_spec(wuk.shape), _whole_spec(wuvt.shape),
                  _row_spec(HEAD_LANES), _row_spec(HEAD_LANES)],
        out_specs=[_row_spec(hw), _row_spec(hw), _vt_out_spec(2)],
        out_shape=[jax.ShapeDtypeStruct((b, s, hw), BF16),
                   jax.ShapeDtypeStruct((b, s, hw), BF16),
                   jax.ShapeDtypeStruct(_vt_shape(b, s, 2), BF16)],
        compiler_params=pltpu.CompilerParams(vmem_limit_bytes=VMEM_LIMIT),
        name="pre1",
    )(x, mod, mod, wd, qn, kvn, wuq, wuk, wuvt, cos_t, sin_t)


def _t5_bucket(rel):
    nb = REL_BUCKETS // 2
    max_exact = nb // 2
    n_log = nb - max_exact
    thresholds = []
    for k in range(1, n_log):
        n = max_exact
        while n ** n_log * max_exact ** k < REL_MAX_DIST ** k * max_exact ** n_log:
            n += 1
        thresholds.append(n)
    n = np.abs(rel)
    large = max_exact + sum((n >= thr).astype(np.int32) for thr in thresholds)
    return (rel > 0).astype(np.int32) * nb + np.where(n < max_exact, n, large)


def _near_bucket_tiles():
    kk = np.arange(ATTN_TILE, dtype=np.int32)[:, None]
    qq = np.arange(ATTN_TILE, dtype=np.int32)[None, :]
    diag = np.where(kk // CHUNK <= qq // CHUNK, _t5_bucket(kk - qq), -1)
    prev = _t5_bucket(kk - ATTN_TILE - qq)
    return jnp.asarray(np.stack([diag, prev]).astype(np.int32))


def _bias_kernel(tab_ref, idx_ref, o_ref):
    h = pl.program_id(0)
    idx = idx_ref[...]
    val = jnp.zeros(idx.shape, F32)
    for bkt in range(REL_BUCKETS):
        val = jnp.where(idx == bkt, tab_ref[bkt, h], val)
    far = tab_ref[REL_BUCKETS // 2 - 1, h]
    o_ref[0] = jnp.where(idx < 0, NEG_BIG, (val - far) * LOG2E)


def _near_bias(rel_table):
    idx = _near_bucket_tiles()
    return pl.pallas_call(
        _bias_kernel,
        grid=(DA_HEADS,),
        in_specs=[pl.BlockSpec(memory_space=pltpu.SMEM),
                  pl.BlockSpec(idx.shape, lambda h: (0, 0, 0))],
        out_specs=pl.BlockSpec((1,) + idx.shape, lambda h: (h, 0, 0, 0)),
        out_shape=jax.ShapeDtypeStruct((DA_HEADS,) + idx.shape, F32),
        name="near_bias",
    )(rel_table, idx)


def _near_mask():
    kk = jnp.arange(ATTN_TILE, dtype=jnp.int32)[:, None]
    qq = jnp.arange(ATTN_TILE, dtype=jnp.int32)[None, :]
    return jnp.where(kk // CHUNK <= qq // CHUNK, 0.0, NEG_BIG).astype(F32)[None, None]


def _tile_tables(nq, n_near_kinds):
    ti, tj, ta = [], [], []
    for kind in range(n_near_kinds):
        for i in range(kind, nq):
            ti.append(i), tj.append(i - kind), ta.append(kind)
    n_near = len(ti)
    for j in range(nq - n_near_kinds):
        for i in range(j + n_near_kinds, nq):
            ti.append(i), tj.append(j), ta.append(0)
    return np.asarray(ti, np.int32), np.asarray(tj, np.int32), np.asarray(ta, np.int32), n_near


def _attn_kernel(*refs, diff, lam_init, n_near, n_tiles, tables):
    q_ref, k_ref, vt_ref, near_ref = refs[:4]
    refs = refs[4:]
    if diff:
        lam_ref, subw_ref = refs[:2]
        refs = refs[2:]
    o_ref, s_buf, p_buf, mx_buf, al_buf, m_scr, acc = refs[:7]
    t = ATTN_TILE
    unroll = s_buf.shape[0]
    nq = acc.shape[1]
    acc_rows = acc.shape[2]
    dv = acc_rows - SUM_ROWS

    if diff:
        qm = refs[7]
        q_all = q_ref[0]
        lane = lax.broadcasted_iota(jnp.int32, q_all.shape, 1)
        zero = jnp.zeros_like(q_all)
        qm[0] = jnp.where(lane < DA_HEAD_DIM, q_all, zero)
        qm[1] = jnp.where(lane >= DA_HEAD_DIM, q_all, zero)

        lam = (jnp.exp(jnp.sum(lam_ref[0:1] * lam_ref[1:2], axis=-1, keepdims=True))
               - jnp.exp(jnp.sum(lam_ref[2:3] * lam_ref[3:4], axis=-1, keepdims=True)) + lam_init)
        sub_w = subw_ref[...] * (1.0 - lam_init)

    ti_tab, tj_tab, ta_tab = ([int(v) for v in tab] for tab in tables)

    def rows(idx):
        return pl.ds(idx * t, t)

    def first_tile(n):
        return n < nq

    def q_map(mp, i):
        if diff:
            return qm[mp, rows(i), :]
        return q_ref[0, rows(i), mp * HEAD_LANES:(mp + 1) * HEAD_LANES]

    def score_dot(n, mp):
        i, j = ti_tab[n], tj_tab[n]
        k_m = k_ref[0, rows(j), :] if diff else k_ref[0, rows(j), mp * HEAD_LANES:(mp + 1) * HEAD_LANES]
        return _dot_nt(k_m, q_map(mp, i))

    def score_store(n, u, mp, s, near):
        s_buf[u, mp] = s
        if near:
            s = s_buf[u, mp] + near_ref[0, ta_tab[n]]
            s_buf[u, mp] = s
        mx_buf[u, mp] = jnp.broadcast_to(jnp.max(s, axis=0, keepdims=True), (SUBLANES, t))

    def per_query(x, stat):
        return x.reshape(x.shape[0] // SUBLANES, SUBLANES, t), stat[None]

    def exps(n, u, mp):
        i = ti_tab[n]
        if first_tile(n):
            m_new = mx_buf[u, mp]
        else:
            m_old = m_scr[mp, i]
            m_new = jnp.maximum(m_old, mx_buf[u, mp])
            al_buf[u, mp] = jnp.exp2(m_old - m_new)
        s3, m3 = per_query(s_buf[u, mp], m_new)
        p = jnp.exp2(s3 - m3).reshape(t, t)
        m_scr[mp, i] = m_new
        p_buf[u, mp] = p.astype(BF16)

    def pv_dot(n, u, mp):
        j = tj_tab[n]
        vt_m = vt_ref[0, 0, j] if diff else vt_ref[0, 0, j, mp * acc_rows:(mp + 1) * acc_rows, :]
        alpha = None if first_tile(n) else al_buf[u, mp]
        return _dot(vt_m, p_buf[u, mp]), alpha

    def acc_update(n, mp, pv, alpha):
        i = ti_tab[n]
        if first_tile(n):
            acc[mp, i] = pv
        else:
            a3, al3 = per_query(acc[mp, i], alpha)
            acc[mp, i] = (al3 * a3).reshape(acc_rows, t) + pv

    items = [(u, mp) for u in range(unroll) for mp in range(2)]

    def step(trip, near_flags, has1=True, has2=True, has3=True):
        pending = {}

        def issue(kk):
            u, mp = items[kk]
            pv = pv_dot(unroll * (trip - 2) + u, u, mp) if has3 else None
            s = score_dot(unroll * trip + u, mp) if has1 else None
            pending[kk] = (s, pv)

        def retire(kk):
            u, mp = items[kk]
            s, pv = pending.pop(kk)
            if has1:
                score_store(unroll * trip + u, u, mp, s, near_flags[u])
            if has3:
                acc_update(unroll * (trip - 2) + u, mp, *pv)

        for kk in range(min(DOT_LEAD, len(items))):
            issue(kk)
        for kk, (u, mp) in enumerate(items):
            if has2:
                exps(unroll * (trip - 1) + u, u, mp)
            if kk + DOT_LEAD < len(items):
                issue(kk + DOT_LEAD)
            if kk + DOT_LEAD - RETIRE_LAG >= 0:
                retire(kk + DOT_LEAD - RETIRE_LAG)
        for kk in sorted(pending):
            retire(kk)

    def finalize(i):
        o_a = acc[0, i, :dv] / acc[0, i, dv:dv + 1]
        o_b = acc[1, i, :dv] / acc[1, i, dv:dv + 1]
        if diff:
            o = o_a - lam * o_b
            o = o * lax.rsqrt(jnp.mean(o * o, axis=0, keepdims=True) + RMS_EPS) * sub_w
        else:
            o = jnp.concatenate([o_a, o_b], axis=0)
        o_ref[0, rows(i), :] = o.T.astype(BF16)

    n_trips = n_tiles // unroll
    last_step = {i: n // unroll + 2 for n, i in enumerate(ti_tab)}
    for trip in range(n_trips + 2):
        flags = [unroll * trip + u < n_near for u in range(unroll)]
        step(trip, flags, has1=trip < n_trips, has2=1 <= trip <= n_trips, has3=trip >= 2)
        for i in sorted(i for i, last in last_step.items() if last == trip):
            finalize(i)


def _attention(q, k, vt, near, lam_vecs=None, subln_w=None, *, diff, lam_init=0.0):
    b, s, _ = q.shape
    t = ATTN_TILE
    nq = s // t
    heads = DA_HEADS
    width = HEAD_LANES if diff else 2 * HEAD_LANES
    acc_rows = (HEAD_LANES if diff else MLA_V) + SUM_ROWS
    assert vt.shape[3] == (1 if diff else 2) * acc_rows
    near_heads = near.shape[0]
    ti, tj, ta, n_near = _tile_tables(nq, near.shape[1])
    n_tiles = len(ti)
    assert n_tiles % ATTN_UNROLL == 0 and n_tiles // ATTN_UNROLL >= 2
    in_specs = [
        pl.BlockSpec((1, s, width), lambda bi, h: (bi, 0, h)),
        pl.BlockSpec((1, s, width), lambda bi, h: (bi, 0, h)),
        pl.BlockSpec((1, 1, nq, vt.shape[3], t), lambda bi, h: (bi, h, 0, 0, 0)),
        pl.BlockSpec((1,) + near.shape[1:], lambda bi, h: (h if near_heads > 1 else 0, 0, 0, 0)),
    ]
    args = [q, k, vt, near]
    if diff:
        in_specs += [pl.BlockSpec(lam_vecs.shape, lambda bi, h: (0, 0)),
                     pl.BlockSpec(subln_w.shape, lambda bi, h: (0, 0))]
        args += [lam_vecs, subln_w]
    un = ATTN_UNROLL
    scratch = [pltpu.VMEM((un, 2, t, t), F32), pltpu.VMEM((un, 2, t, t), BF16),
               pltpu.VMEM((un, 2, SUBLANES, t), F32), pltpu.VMEM((un, 2, SUBLANES, t), F32),
               pltpu.VMEM((2, nq, SUBLANES, t), F32), pltpu.VMEM((2, nq, acc_rows, t), F32)]
    if diff:
        scratch.append(pltpu.VMEM((2, s, HEAD_LANES), BF16))
    return pl.pallas_call(
        functools.partial(_attn_kernel, diff=diff, lam_init=lam_init, n_near=n_near, n_tiles=n_tiles,
                          tables=(ti, tj, ta)),
        grid=(b, heads),
        in_specs=in_specs,
        out_specs=pl.BlockSpec((1, s, HEAD_LANES), lambda bi, h: (bi, 0, h)),
        out_shape=jax.ShapeDtypeStruct((b, s, heads * HEAD_LANES), BF16),
        scratch_shapes=scratch,
        compiler_params=pltpu.CompilerParams(vmem_limit_bytes=VMEM_LIMIT),
        name="attn_diff" if diff else "attn_mla",
    )(*args)


def _post_ffn_kernel(o_ref, x_ref, g1_ref, wo_ref, ln1g_ref, ln1b_ref, sc_ref, sh_ref, g2_ref,
                     w1_ref, w2_ref, ln2g_ref, ln2b_ref, out_ref):
    half = ROW_TILE // 2
    halves = (slice(0, half), slice(half, ROW_TILE))
    n_chunks = D_FF // FF_CHUNK

    def attn_out(r):
        return _dot(o_ref[0, r, :], wo_ref[...])

    def norm1(r, y):
        z = DEEPNORM_ALPHA * x_ref[0, r, :] + (1.0 + g1_ref[...]) * y
        x1 = _layer_norm_rows(z, ln1g_ref[...], ln1b_ref[...])
        return x1, (x1 * (1.0 + sc_ref[...]) + sh_ref[...]).astype(BF16)

    def mlp_chunk(u, y, c):
        sl = slice(c * FF_CHUNK, (c + 1) * FF_CHUNK)
        h = jnp.maximum(_dot(u, w1_ref[:, sl]), 0.0)
        upd = _dot((h * h).astype(BF16), w2_ref[sl, :])
        return upd if y is None else y + upd

    def norm2(r, x1, y):
        z = DEEPNORM_ALPHA * x1 + (1.0 + g2_ref[...]) * y
        out_ref[0, r, :] = _layer_norm_rows(z, ln2g_ref[...], ln2b_ref[...])

    ra, rb = halves
    ya, yb = attn_out(ra), attn_out(rb)
    x1a, ua = norm1(ra, ya)
    fa = mlp_chunk(ua, None, 0)
    x1b, ub = norm1(rb, yb)
    for c in range(1, n_chunks):
        fa = mlp_chunk(ua, fa, c)
    fb = mlp_chunk(ub, None, 0)
    norm2(ra, x1a, fa)
    for c in range(1, n_chunks):
        fb = mlp_chunk(ub, fb, c)
    norm2(rb, x1b, fb)


def _post_ffn(o, x, mod, layer, wo, w1, w2, ln_g, ln_b):
    b, s, d = x.shape
    resident = functools.partial(pl.BlockSpec, index_map=lambda bi, i: (0, 0), pipeline_mode=pl.Buffered(1))
    return pl.pallas_call(
        _post_ffn_kernel,
        grid=(b, s // ROW_TILE),
        in_specs=[_row_spec(d), _row_spec(d), _mod_spec(layer, 2), resident((d, d)),
                  _whole_spec((1, d)), _whole_spec((1, d)), _mod_spec(layer, 4), _mod_spec(layer, 3),
                  _mod_spec(layer, 5), resident((d, D_FF)), resident((D_FF, d)),
                  _whole_spec((1, d)), _whole_spec((1, d))],
        out_specs=_row_spec(d),
        out_shape=jax.ShapeDtypeStruct((b, s, d), F32),
        compiler_params=pltpu.CompilerParams(vmem_limit_bytes=VMEM_LIMIT),
        name=f"post_ffn{layer}",
    )(o, x, mod, wo, ln_g[0][None], ln_b[0][None], mod, mod, mod, w1, w2, ln_g[1][None], ln_b[1][None])


def _rope_tables(pos_offset, s):
    half = MLA_ROPE // 2
    pos = pos_offset[:, None] + jnp.arange(s, dtype=jnp.int32)[None, :]
    inv = ROPE_THETA ** (-jnp.arange(half, dtype=F32) / half)
    ang = pos.astype(F32)[..., None] * inv
    cos = jnp.cos(ang)
    sin = jnp.sin(ang)
    b = pos.shape[0]
    ones = jnp.ones((b, s, MLA_NOPE), F32)
    zeros_tail = jnp.zeros((b, s, HEAD_LANES - MLA_NOPE - MLA_ROPE), F32)
    cos_t = jnp.concatenate([ones, cos, cos, zeros_tail], axis=-1)
    sin_t = jnp.concatenate([jnp.zeros_like(ones), -sin, sin, zeros_tail], axis=-1)
    return cos_t, sin_t


def _pad_heads(w, per_head, lead):
    w = w.reshape(lead, MLA_HEADS, per_head)
    w = jnp.pad(w, ((0, 0), (0, 0), (0, HEAD_LANES - per_head)))
    return w.reshape(lead, MLA_HEADS * HEAD_LANES)


def kernel(x, c, pos_offset, ada_w, ada_b, ln_g, ln_b, rel_table, da_w_qkv, da_w_o, da_lam_q1, da_lam_k1, da_lam_q2, da_lam_k2, da_subln_w, mla_w_down, mla_q_norm_w, mla_w_uq, mla_kv_norm_w, mla_w_ukv, mla_w_o, ffn_w1, ffn_w2):
    b, s, d = x.shape
    assert d == D_MODEL and s % ROW_TILE == 0 and ROW_TILE % ATTN_TILE == 0 and ATTN_TILE % CHUNK == 0

    mod = _modulation(c, ada_w, ada_b)

    w_qkv = da_w_qkv[0].astype(BF16)
    q, k, vt = _pre0(x, mod, w_qkv[:, :d], w_qkv[:, d:2 * d], w_qkv[:, 2 * d:].T)
    lam_vecs = jnp.stack([da_lam_q1[0], da_lam_k1[0], da_lam_q2[0], da_lam_k2[0]])
    lam_init = 0.8 - 0.6 * math.exp(-0.3 * 0)
    o = _attention(q, k, vt, _near_bias(rel_table), lam_vecs, da_subln_w[0].reshape(-1, 1),
                   diff=True, lam_init=lam_init)
    x = _post_ffn(o, x, mod, 0, da_w_o[0].astype(BF16), ffn_w1[0].astype(BF16), ffn_w2[0].astype(BF16),
                  ln_g[0], ln_b[0])

    n_lat = MLA_Q_LORA + MLA_KV_LORA
    w_down = mla_w_down[0]
    w_down = jnp.concatenate(
        [w_down[:, :n_lat], jnp.zeros((d, MLA_NOPE), F32), w_down[:, n_lat:],
         jnp.zeros((d, HEAD_LANES - MLA_NOPE - MLA_ROPE), F32)], axis=-1).astype(BF16)
    w_uq = _pad_heads(mla_w_uq[0], MLA_NOPE + MLA_ROPE, MLA_Q_LORA).astype(BF16)
    w_ukv = mla_w_ukv[0].reshape(MLA_KV_LORA, MLA_HEADS, MLA_NOPE + MLA_V)
    w_uk = _pad_heads(w_ukv[..., :MLA_NOPE].reshape(MLA_KV_LORA, -1), MLA_NOPE, MLA_KV_LORA).astype(BF16)
    w_uvt = w_ukv[..., MLA_NOPE:].reshape(MLA_KV_LORA, -1).T.astype(BF16)
    cos_t, sin_t = _rope_tables(pos_offset, s)
    q, k, vt = _pre1(x, mod, w_down, mla_q_norm_w[0][None], mla_kv_norm_w[0][None],
                     w_uq, w_uk, w_uvt, cos_t, sin_t)
    o = _attention(q, k, vt, _near_mask(), diff=False)
    return _post_ffn(o, x, mod, 1, mla_w_o[0].astype(BF16), ffn_w1[1].astype(BF16), ffn_w2[1].astype(BF16),
                     ln_g[1], ln_b[1])
```
